```python
import jax, jax.numpy as jnp
from jax import lax
import numpy as np

D_MODEL = 1024
BATCH = 8
SEQ = 2048
DEPTH = 4
DEC_BATCH = 128
DEC_SEQ = 1
PAST_LEN = 2048
PAGE_SIZE = 128

HD_FOX = 64
W_FOX = D_MODEL // 2
H_FOX = W_FOX // HD_FOX
H_GLA = 4
DV_GLA = (D_MODEL // 2) // H_GLA
DK_GLA = DV_GLA // 2
WK_GLA = H_GLA * DK_GLA
WV_GLA = H_GLA * DV_GLA
MIX_WIDTH = W_FOX + WV_GLA
GATE_RANK = 16
GATE_NORM = 16.0
Q_BLOCK = 128
GLA_CHUNK = 16
EPS = 1e-5
DEEPNORM_ALPHA = (2.0 * DEPTH) ** 0.25
DEEPNORM_BETA = (8.0 * DEPTH) ** -0.25
SPLITS = (W_FOX, W_FOX, W_FOX, H_FOX, W_FOX, WK_GLA, WK_GLA, WV_GLA, GATE_RANK, WV_GLA)
D_IN = sum(SPLITS)
SPLIT_IDX = tuple(int(i) for i in np.cumsum(SPLITS)[:-1])

kernel_name = "fox_gla_parallel_heads_deepnorm_adaln_step"


def layer_norm(x, gain=None, bias=None):
    xf = x.astype(jnp.float32)
    mu = jnp.mean(xf, -1, keepdims=True)
    var = jnp.mean(jnp.square(xf - mu), -1, keepdims=True)
    y = (xf - mu) * lax.rsqrt(var + EPS)
    if gain is not None:
        y = y * gain.astype(jnp.float32) + bias.astype(jnp.float32)
    return y.astype(x.dtype)


def branch_inputs(x, c, w_ada, b_ada, w_in, b_f, w_gk, b_gk):
    B, T, _ = x.shape
    shift, scale, gate = jnp.split(jax.nn.silu(c) @ w_ada + b_ada, 3, axis=-1)
    h = layer_norm(x) * (1.0 + scale[:, None]) + shift[:, None]
    qf, kf, vf, fl, gf, qg, kg, vg, gl, gg = jnp.split(h @ w_in, SPLIT_IDX, axis=-1)
    logf = jax.nn.log_sigmoid((fl + b_f).astype(jnp.float32))
    log_a = jax.nn.log_sigmoid((gl @ w_gk + b_gk).astype(jnp.float32)) / GATE_NORM
    fox = (qf.reshape(B, T, H_FOX, HD_FOX), kf.reshape(B, T, H_FOX, HD_FOX),
           vf.reshape(B, T, H_FOX, HD_FOX), logf)
    gla = (qg.reshape(B, T, H_GLA, DK_GLA) * (DK_GLA ** -0.5), kg.reshape(B, T, H_GLA, DK_GLA),
           vg.reshape(B, T, H_GLA, DV_GLA), log_a.reshape(B, T, H_GLA, DK_GLA))
    return fox, gla, gf, gg, gate[:, None]


def branch_merge(x, o_fox, o_gla, gf, gg, gate, gla_gain, w_out, ln_g, ln_b):
    B, T, _ = x.shape
    of = o_fox.reshape(B, T, W_FOX).astype(x.dtype)
    ogf = o_gla.astype(jnp.float32)
    ogf = ogf * lax.rsqrt(jnp.mean(jnp.square(ogf), -1, keepdims=True) + EPS) * gla_gain.astype(jnp.float32)
    og = ogf.reshape(B, T, WV_GLA).astype(x.dtype)
    mix = jnp.concatenate([of * jax.nn.silu(gf), og * jax.nn.silu(gg)], axis=-1)
    return layer_norm(DEEPNORM_ALPHA * x + gate * (mix @ w_out), ln_g, ln_b)


def fox_prompt(q, k, v, logf):
    B, S = q.shape[:2]
    cum_t = jnp.transpose(jnp.cumsum(logf, axis=1), (0, 2, 1))
    k_pos = jnp.arange(S)
    scale = HD_FOX ** -0.5

    def block(i):
        start = i * Q_BLOCK
        qb = lax.dynamic_slice_in_dim(q, start, Q_BLOCK, axis=1)
        cq = lax.dynamic_slice_in_dim(cum_t, start, Q_BLOCK, axis=2)
        s = jnp.einsum('bqhd,bkhd->bhqk', qb, k).astype(jnp.float32) * scale
        s = s + cq[..., None] - cum_t[:, :, None, :]
        q_pos = start + jnp.arange(Q_BLOCK)
        s = jnp.where(k_pos[None, :] <= q_pos[:, None], s, -jnp.inf)
        p = jax.nn.softmax(s, axis=-1).astype(v.dtype)
        return jnp.einsum('bhqk,bkhd->bqhd', p, v)

    out = lax.map(block, jnp.arange(S // Q_BLOCK))
    return jnp.transpose(out, (1, 0, 2, 3, 4)).reshape(B, S, H_FOX, HD_FOX)


def fox_sample(q, k, v, logf, k_past, v_past, logf_past):
    P, T = k_past.shape[1], q.shape[1]
    k_all = jnp.concatenate([k_past.astype(k.dtype), k], axis=1)
    v_all = jnp.concatenate([v_past.astype(v.dtype), v], axis=1)
    cum = jnp.cumsum(jnp.concatenate([logf_past.astype(jnp.float32), logf], axis=1), axis=1)
    cum_t = jnp.transpose(cum, (0, 2, 1))
    s = jnp.einsum('bqhd,bkhd->bhqk', q, k_all).astype(jnp.float32) * (HD_FOX ** -0.5)
    s = s + cum_t[:, :, P:, None] - cum_t[:, :, None, :]
    mask = jnp.arange(P + T)[None, :] <= (P + jnp.arange(T))[:, None]
    s = jnp.where(mask, s, -jnp.inf)
    p = jax.nn.softmax(s, axis=-1).astype(v_all.dtype)
    return jnp.einsum('bhqk,bkhd->bqhd', p, v_all)


def gla_prompt(q, k, v, log_a):
    B, S = q.shape[:2]
    N, C = S // GLA_CHUNK, GLA_CHUNK
    f32 = jnp.float32
    qc = q.astype(f32).reshape(B, N, C, H_GLA, DK_GLA)
    kc = k.astype(f32).reshape(B, N, C, H_GLA, DK_GLA)
    vc = v.astype(f32).reshape(B, N, C, H_GLA, DV_GLA)
    b = jnp.cumsum(log_a.reshape(B, N, C, H_GLA, DK_GLA), axis=2)
    causal = jnp.tril(jnp.ones((C, C), dtype=bool))
    diff = b[:, :, :, None] - b[:, :, None, :]
    decay = jnp.exp(jnp.where(causal[None, None, :, :, None, None], diff, -jnp.inf))
    A = jnp.einsum('bntshk,bnthk,bnshk->bnhts', decay, qc, kc)
    o_intra = jnp.einsum('bnhts,bnshv->bnthv', A, vc)
    b_last = b[:, :, -1]
    dS = jnp.einsum('bnshk,bnshv->bnhkv', kc * jnp.exp(b_last[:, :, None] - b), vc)
    a_chunk = jnp.exp(b_last)

    def step(state, inp):
        a, d = inp
        return a[..., None] * state + d, state

    s0 = jnp.zeros((B, H_GLA, DK_GLA, DV_GLA), f32)
    s_final, s_prev = lax.scan(step, s0, (jnp.moveaxis(a_chunk, 1, 0), jnp.moveaxis(dS, 1, 0)))
    s_prev = jnp.moveaxis(s_prev, 0, 1)
    o_inter = jnp.einsum('bnthk,bnhkv->bnthv', qc * jnp.exp(b), s_prev)
    o = (o_intra + o_inter).reshape(B, S, H_GLA, DV_GLA)
    return o, s_final


def gla_sample(q, k, v, log_a, s0):
    f32 = jnp.float32

    def step(state, inp):
        qt, kt, vt, at = inp
        state = jnp.exp(at)[..., None] * state + kt[..., :, None] * vt[..., None, :]
        return state, jnp.einsum('bhk,bhkv->bhv', qt, state)

    xs = tuple(jnp.moveaxis(t.astype(f32), 1, 0) for t in (q, k, v, log_a))
    s_final, o = lax.scan(step, s0.astype(f32), xs)
    return jnp.moveaxis(o, 0, 1), s_final


def setup_inputs(seed: int = 0) -> dict:
    key = jax.random.key(seed)
    ks = jax.random.split(key, 20)
    n_pages = PAST_LEN // PAGE_SIZE
    n_used = DEC_BATCH * n_pages
    n_pool = n_used + max(1, n_used // 4)

    def nrm(k, shape, s=1.0):
        return s * jax.random.normal(k, shape, jnp.float32)

    page_table = jax.random.permutation(ks[0], n_pool)[:n_used].reshape(DEC_BATCH, n_pages).astype(jnp.int32)
    return {
        "x_prompt": nrm(ks[1], (BATCH, SEQ, D_MODEL)),
        "x_sample": nrm(ks[2], (DEC_BATCH, DEC_SEQ, D_MODEL)),
        "cache_k": nrm(ks[3], (DEPTH, n_pool, PAGE_SIZE, H_FOX, HD_FOX)),
        "cache_v": nrm(ks[4], (DEPTH, n_pool, PAGE_SIZE, H_FOX, HD_FOX)),
        "cache_logf": jax.nn.log_sigmoid(nrm(ks[5], (DEPTH, n_pool, PAGE_SIZE, H_FOX)) + 2.5),
        "state_gla": nrm(ks[6], (DEPTH, DEC_BATCH, H_GLA, DK_GLA, DV_GLA), 0.3),
        "page_table": page_table,
        "c_prompt": nrm(ks[7], (BATCH, D_MODEL)),
        "c_sample": nrm(ks[8], (DEC_BATCH, D_MODEL)),
        "w_ada": nrm(ks[9], (DEPTH, D_MODEL, 3 * D_MODEL), 0.5 * D_MODEL ** -0.5),
        "b_ada": nrm(ks[10], (DEPTH, 3 * D_MODEL), 0.1),
        "w_in": nrm(ks[11], (DEPTH, D_MODEL, D_IN), D_MODEL ** -0.5),
        "b_f": jax.random.uniform(ks[12], (DEPTH, H_FOX), jnp.float32, 1.0, 4.0),
        "w_gk": nrm(ks[13], (DEPTH, GATE_RANK, WK_GLA), GATE_RANK ** -0.5),
        "b_gk": nrm(ks[14], (DEPTH, WK_GLA), 0.1),
        "gla_gain": 1.0 + nrm(ks[15], (DEPTH, DV_GLA), 0.1),
        "w_out": nrm(ks[16], (DEPTH, MIX_WIDTH, D_MODEL), DEEPNORM_BETA * MIX_WIDTH ** -0.5),
        "ln_g": 1.0 + nrm(ks[17], (DEPTH, D_MODEL), 0.1),
        "ln_b": nrm(ks[18], (DEPTH, D_MODEL), 0.1),
    }


def reference(x_prompt, x_sample, cache_k, cache_v, cache_logf, state_gla, page_table,
              c_prompt, c_sample, w_ada, b_ada, w_in, b_f, w_gk, b_gk, gla_gain, w_out, ln_g, ln_b):
    xp, xs = x_prompt, x_sample
    dec_b = page_table.shape[0]
    kp_l, vp_l, fp_l, sp_l = [], [], [], []
    ks_l, vs_l, fs_l, ss_l = [], [], [], []
    for l in range(DEPTH):
        fox, gla, gf, gg, gate = branch_inputs(xp, c_prompt, w_ada[l], b_ada[l], w_in[l], b_f[l], w_gk[l], b_gk[l])
        o_f = fox_prompt(*fox)
        o_g, s_p = gla_prompt(*gla)
        xp = branch_merge(xp, o_f, o_g, gf, gg, gate, gla_gain[l], w_out[l], ln_g[l], ln_b[l])
        kp_l.append(fox[1]); vp_l.append(fox[2]); fp_l.append(fox[3]); sp_l.append(s_p)
        fox, gla, gf, gg, gate = branch_inputs(xs, c_sample, w_ada[l], b_ada[l], w_in[l], b_f[l], w_gk[l], b_gk[l])
        k_past = cache_k[l][page_table].reshape(dec_b, -1, H_FOX, HD_FOX)
        v_past = cache_v[l][page_table].reshape(dec_b, -1, H_FOX, HD_FOX)
        f_past = cache_logf[l][page_table].reshape(dec_b, -1, H_FOX)
        o_f = fox_sample(*fox, k_past, v_past, f_past)
        o_g, s_s = gla_sample(*gla, state_gla[l])
        xs = branch_merge(xs, o_f, o_g, gf, gg, gate, gla_gain[l], w_out[l], ln_g[l], ln_b[l])
        ks_l.append(fox[1]); vs_l.append(fox[2]); fs_l.append(fox[3]); ss_l.append(s_s)
    return (xp, xs,
            jnp.stack(kp_l), jnp.stack(vp_l), jnp.stack(fp_l), jnp.stack(sp_l),
            jnp.stack(ks_l), jnp.stack(vs_l), jnp.stack(fs_l), jnp.stack(ss_l))
```

```python
import functools

import jax
import jax.numpy as jnp
from jax import lax
from jax.experimental import pallas as pl
from jax.experimental.pallas import tpu as pltpu

D_MODEL = 1024
DEPTH = 4
PAGE_SIZE = 128
HD_FOX = 64
W_FOX = D_MODEL // 2
H_FOX = W_FOX // HD_FOX
H_GLA = 4
DV_GLA = (D_MODEL // 2) // H_GLA
DK_GLA = DV_GLA // 2
WK_GLA = H_GLA * DK_GLA
WV_GLA = H_GLA * DV_GLA
GATE_RANK = 16
GATE_NORM = 16.0
EPS = 1e-5
DEEPNORM_ALPHA = (2.0 * DEPTH) ** 0.25
_SPLITS = (W_FOX, W_FOX, W_FOX, H_FOX, W_FOX, WK_GLA, WK_GLA, WV_GLA, GATE_RANK, WV_GLA)
_OFF = [0]
for _s in _SPLITS:
    _OFF.append(_OFF[-1] + _s)
(_QF, _KF, _VF, _FL, _GF, _QG, _KG, _VG, _GL, _GG) = _OFF[:-1]

LANES = 128
SUB_BF16 = 16
FOX_SCALE = HD_FOX ** -0.5
GLA_SCALE = DK_GLA ** -0.5
GLA_SUB = 16
GLA_CHUNK = 128
VMEM_LIMIT = 56 * 1024 * 1024

F32 = jnp.float32
BF16 = jnp.bfloat16


def _cparams(sem):
    return pltpu.CompilerParams(dimension_semantics=sem, vmem_limit_bytes=VMEM_LIMIT)


def _split3(x):
    hi = x.astype(BF16)
    r = x - hi.astype(F32)
    mid = r.astype(BF16)
    lo = (r - mid.astype(F32)).astype(BF16)
    return hi, mid, lo


def _log_sigmoid(x):
    return -(jnp.maximum(-x, 0.0) + jnp.log1p(jnp.exp(-jnp.abs(x))))


def _silu(x):
    return x * (1.0 / (1.0 + jnp.exp(-x)))


def _mm(a, b):
    return jnp.dot(a, b, preferred_element_type=F32)


def _mm_nt(a, b):
    return lax.dot_general(a, b, (((1,), (1,)), ((), ())), preferred_element_type=F32)


def _mm_tn(a, b):
    return lax.dot_general(a, b, (((0,), (0,)), ((), ())), preferred_element_type=F32)


def _ada_kernel(c_ref, w_ref, b_ref, o_ref):
    c = c_ref[...]
    o_ref[...] = _mm(_silu(c).astype(BF16), w_ref[...].astype(BF16)) + b_ref[...]


def _ada_all(c_all, w_ada, b_ada):
    n = c_all.shape[0]
    nb = 3 * D_MODEL // D_MODEL
    return pl.pallas_call(
        _ada_kernel,
        grid=(DEPTH, nb),
        in_specs=[pl.BlockSpec((n, D_MODEL), lambda l, j: (0, 0)),
                  pl.BlockSpec((None, D_MODEL, D_MODEL), lambda l, j: (l, 0, j)),
                  pl.BlockSpec((None, 1, D_MODEL), lambda l, j: (l, 0, j))],
        out_specs=pl.BlockSpec((None, n, D_MODEL), lambda l, j: (l, 0, j)),
        out_shape=jax.ShapeDtypeStruct((DEPTH, n, 3 * D_MODEL), F32),
        compiler_params=_cparams(("arbitrary", "arbitrary")),
        name="ada_mod",
    )(c_all, w_ada, b_ada.reshape(DEPTH, 1, 3 * D_MODEL))


def _modulated_ln(x, shift, scale):
    mu = jnp.mean(x, axis=-1, keepdims=True)
    xc = x - mu
    var = jnp.mean(xc * xc, axis=-1, keepdims=True)
    return (xc * lax.rsqrt(var + EPS)) * (1.0 + scale) + shift


_R_QF, _R_GF, _R_QG, _R_KG, _R_VG, _R_GG, _R_GL = 0, 512, 1024, 1280, 1536, 2048, 2560
_R_COLS = 2688
_T_KF, _T_VF, _T_FL = 0, 512, 1024
_T_ROWS = 1024 + SUB_BF16


def _inproj_prompt_kernel(x_ref, shift_ref, scale_ref, wr_ref, wt_ref, bf_ref, wgk_ref, bgk_ref,
                          qa_ref, kT_ref, vT_ref, ka_ref, vTb_ref, logfT_ref,
                          gf_ref, qg_ref, kg_ref, vg_ref, la_ref, gg_ref, carry_ref):
    i = pl.program_id(1)
    tm = x_ref.shape[0]
    h = _modulated_ln(x_ref[...], shift_ref[...], scale_ref[...]).astype(BF16)
    r = _mm(h, wr_ref[...])
    t = _mm_nt(wt_ref[...], h)

    lane = lax.broadcasted_iota(jnp.int32, (tm, LANES), 1)
    tail = jnp.where(lane < HD_FOX + 3, 1.0, 0.0)
    for hh in range(H_FOX):
        blk = r[:, _R_QF + (hh // 2) * LANES:_R_QF + (hh // 2 + 1) * LANES]
        if hh % 2:
            blk = pltpu.roll(blk, HD_FOX, axis=1)
        qa_ref[hh] = jnp.where(lane < HD_FOX, blk, tail).astype(BF16)

    kT = t[_T_KF:_T_KF + W_FOX]
    vT = t[_T_VF:_T_VF + W_FOX]
    kT_ref[...] = kT
    vT_ref[...] = vT
    vTb_ref[...] = vT.astype(BF16)

    logfT = _log_sigmoid(t[_T_FL:_T_FL + SUB_BF16] + bf_ref[...])
    logfT_ref[...] = logfT[0:H_FOX]

    @pl.when(i == 0)
    def _():
        carry_ref[...] = jnp.zeros_like(carry_ref)

    row = lax.broadcasted_iota(jnp.int32, (tm, tm), 0)
    col = lax.broadcasted_iota(jnp.int32, (tm, tm), 1)
    upper = jnp.where(row <= col, 1.0, 0.0).astype(BF16)
    hi, mid, lo = _split3(logfT)
    carry = carry_ref[...][:, 0:1]
    cumT = _mm(hi, upper) + _mm(mid, upper) + _mm(lo, upper) + carry
    carry_ref[...] = jnp.broadcast_to(carry + jnp.sum(logfT, axis=1, keepdims=True), carry_ref.shape)

    chi, cmid, clo = _split3(cumT)
    sub = lax.broadcasted_iota(jnp.int32, (SUB_BF16, tm), 0)
    zeros = jnp.zeros((LANES - HD_FOX - SUB_BF16, tm), BF16)
    for hh in range(H_FOX):
        extra = jnp.where(sub == 0, -chi[hh:hh + 1].astype(F32),
                          jnp.where(sub == 1, -cmid[hh:hh + 1].astype(F32),
                                    jnp.where(sub == 2, -clo[hh:hh + 1].astype(F32), 0.0)))
        ka_ref[hh, 0:HD_FOX, :] = kT[hh * HD_FOX:(hh + 1) * HD_FOX].astype(BF16)
        ka_ref[hh, HD_FOX:HD_FOX + SUB_BF16, :] = extra.astype(BF16)
        ka_ref[hh, HD_FOX + SUB_BF16:LANES, :] = zeros

    gf_ref[...] = r[:, _R_GF:_R_GF + W_FOX]
    qg_ref[...] = r[:, _R_QG:_R_QG + WK_GLA]
    kg_ref[...] = r[:, _R_KG:_R_KG + WK_GLA]
    vg_ref[...] = r[:, _R_VG:_R_VG + WV_GLA]
    gg_ref[...] = r[:, _R_GG:_R_GG + WV_GLA]
    gl = r[:, _R_GL:_R_GL + LANES].astype(BF16)
    la_ref[...] = _log_sigmoid(_mm(gl, wgk_ref[...]) + bgk_ref[...]) * (1.0 / GATE_NORM)


def _inproj_prompt(l, x, shift, scale, wr, wt, bf_col, wgk_pad, bgk, tm=256):
    B, S, _ = x.shape
    nt = S // tm
    row = lambda n: pl.BlockSpec((None, tm, n), lambda b, i: (b, i, 0))
    colT = lambda n: pl.BlockSpec((None, n, tm), lambda b, i: (b, 0, i))
    out_shape = [
        jax.ShapeDtypeStruct((B, H_FOX, S, LANES), BF16),
        jax.ShapeDtypeStruct((B, W_FOX, S), F32),
        jax.ShapeDtypeStruct((B, W_FOX, S), F32),
        jax.ShapeDtypeStruct((B, H_FOX, LANES, S), BF16),
        jax.ShapeDtypeStruct((B, W_FOX, S), BF16),
        jax.ShapeDtypeStruct((B, H_FOX, S), F32),
        jax.ShapeDtypeStruct((B, S, W_FOX), F32),
        jax.ShapeDtypeStruct((B, S, WK_GLA), F32),
        jax.ShapeDtypeStruct((B, S, WK_GLA), F32),
        jax.ShapeDtypeStruct((B, S, WV_GLA), F32),
        jax.ShapeDtypeStruct((B, S, WK_GLA), F32),
        jax.ShapeDtypeStruct((B, S, WV_GLA), F32),
    ]
    out_specs = [
        pl.BlockSpec((None, H_FOX, tm, LANES), lambda b, i: (b, 0, i, 0)),
        colT(W_FOX), colT(W_FOX),
        pl.BlockSpec((None, H_FOX, LANES, tm), lambda b, i: (b, 0, 0, i)),
        colT(W_FOX), colT(H_FOX),
        row(W_FOX), row(WK_GLA), row(WK_GLA), row(WV_GLA), row(WK_GLA), row(WV_GLA),
    ]
    in_specs = [
        pl.BlockSpec((None, tm, D_MODEL), lambda b, i: (b, i, 0)),
        pl.BlockSpec((None, 1, D_MODEL), lambda b, i: (b, 0, 0)),
        pl.BlockSpec((None, 1, D_MODEL), lambda b, i: (b, 0, 0)),
        pl.BlockSpec((None, D_MODEL, _R_COLS), lambda b, i: (l, 0, 0)),
        pl.BlockSpec((None, _T_ROWS, D_MODEL), lambda b, i: (l, 0, 0)),
        pl.BlockSpec((None, SUB_BF16, 1), lambda b, i: (l, 0, 0)),
        pl.BlockSpec((None, LANES, WK_GLA), lambda b, i: (l, 0, 0)),
        pl.BlockSpec((None, 1, WK_GLA), lambda b, i: (l, 0, 0)),
    ]
    return pl.pallas_call(
        _inproj_prompt_kernel,
        grid=(B, nt),
        in_specs=in_specs,
        out_specs=out_specs,
        out_shape=out_shape,
        scratch_shapes=[pltpu.VMEM((SUB_BF16, LANES), F32)],
        compiler_params=_cparams(("arbitrary", "arbitrary")),
        name="inproj_prompt",
    )(x, shift, scale, wr, wt, bf_col, wgk_pad, bgk)


def _fox_prompt_kernel(qa_ref, ka_ref, vT_ref, o_ref, m_ref, l_ref, acc_ref):
    i = pl.program_id(2)
    j = pl.program_id(3)
    tq = qa_ref.shape[1]
    tk = ka_ref.shape[2]

    @pl.when(j == 0)
    def _():
        m_ref[...] = jnp.full_like(m_ref, -jnp.inf)
        l_ref[...] = jnp.zeros_like(l_ref)
        acc_ref[...] = jnp.zeros_like(acc_ref)

    def step(masked):
        vT = vT_ref[...]
        lane = lax.broadcasted_iota(jnp.int32, (tq, LANES), 1)
        first = lane < HD_FOX
        alphas, pvs = [], []
        for u in range(2):
            s = _mm(qa_ref[u], ka_ref[u])
            if masked:
                qpos = lax.broadcasted_iota(jnp.int32, (tq, tk), 0)
                kpos = lax.broadcasted_iota(jnp.int32, (tq, tk), 1)
                s = jnp.where(kpos <= qpos, s, -jnp.inf)
            m_prev = m_ref[u]
            m_new = jnp.maximum(m_prev, jnp.max(s, axis=-1, keepdims=True))
            alpha = jnp.exp(m_prev - m_new)
            p = jnp.exp(s - m_new)
            l_ref[u] = alpha * l_ref[u] + jnp.sum(p, axis=-1, keepdims=True)
            m_ref[u] = m_new
            alphas.append(alpha)
            pvs.append(_mm_nt(p.astype(BF16), vT))
        acc_ref[...] = (acc_ref[...] * jnp.where(first, alphas[0], alphas[1])
                        + jnp.where(first, pvs[0], pvs[1]))

    @pl.when(j < i)
    def _():
        step(False)

    @pl.when(j == i)
    def _():
        step(True)
        lane = lax.broadcasted_iota(jnp.int32, (tq, LANES), 1)
        inv = jnp.where(lane < HD_FOX, 1.0 / l_ref[0], 1.0 / l_ref[1])
        o_ref[...] = acc_ref[...] * inv


def _fox_prompt(qa, ka, vTb, t=512):
    B, _, S, _ = qa.shape
    n = S // t
    return pl.pallas_call(
        _fox_prompt_kernel,
        grid=(B, H_FOX // 2, n, n),
        in_specs=[pl.BlockSpec((None, 2, t, LANES), lambda b, hp, i, j: (b, hp, i, 0)),
                  pl.BlockSpec((None, 2, LANES, t), lambda b, hp, i, j: (b, hp, 0, jnp.minimum(j, i))),
                  pl.BlockSpec((None, LANES, t), lambda b, hp, i, j: (b, hp, jnp.minimum(j, i)))],
        out_specs=pl.BlockSpec((None, t, LANES), lambda b, hp, i, j: (b, i, hp)),
        out_shape=jax.ShapeDtypeStruct((B, S, W_FOX), F32),
        scratch_shapes=[pltpu.VMEM((2, t, 1), F32), pltpu.VMEM((2, t, 1), F32),
                        pltpu.VMEM((t, LANES), F32)],
        compiler_params=_cparams(("arbitrary", "arbitrary", "arbitrary", "arbitrary")),
        name="fox_prompt",
    )(qa, ka, vTb)


def _gla_prompt_kernel(q_ref, k_ref, v_ref, la_ref, sel_ref, til_ref, o_ref, sfin_ref,
                       b_scr, k_scr, state_ref):
    c = pl.program_id(1)
    L = q_ref.shape[0]
    nsub = L // GLA_SUB

    @pl.when(c == 0)
    def _():
        state_ref[...] = jnp.zeros_like(state_ref)

    q = q_ref[...]
    k = k_ref[...]
    la = la_ref[...]

    r_i = lax.broadcasted_iota(jnp.int32, (L, L), 0)
    c_i = lax.broadcasted_iota(jnp.int32, (L, L), 1)
    lower = jnp.where(c_i <= r_i, 1.0, 0.0).astype(BF16)
    hi, mid, lo = _split3(la)
    b = _mm(lower, hi) + _mm(lower, mid) + _mm(lower, lo)
    b_scr[...] = b
    k_scr[...] = k

    lane_k = lax.broadcasted_iota(jnp.int32, (L, WK_GLA), 1)
    head_of_lane = lane_k // DK_GLA

    def pair_body(j, a16):
        bj = jnp.concatenate(
            [jnp.broadcast_to(b_scr[pl.ds(GLA_SUB * s + j, 1), :], (GLA_SUB, WK_GLA)) for s in range(nsub)], axis=0)
        kj = jnp.concatenate(
            [jnp.broadcast_to(k_scr[pl.ds(GLA_SUB * s + j, 1), :], (GLA_SUB, WK_GLA)) for s in range(nsub)], axis=0)
        p = q * kj * jnp.exp(jnp.minimum(b - bj, 0.0))
        return a16 + _mm(p.astype(BF16), sel_ref[j])
    a16 = lax.fori_loop(0, GLA_SUB, pair_body, jnp.zeros((L, H_GLA * GLA_SUB), F32))

    t_in = (r_i % GLA_SUB)
    s_in = (c_i % GLA_SUB)
    same_sub = (r_i // GLA_SUB) == (c_i // GLA_SUB)
    diag_mask = same_sub & (s_in <= t_in)

    level_terms = []
    m = GLA_SUB
    while m < L:
        nblk = L // (2 * m)
        bref = jnp.concatenate(
            [jnp.broadcast_to(b_scr[pl.ds(2 * m * s + m - 1, 1), :], (2 * m, WK_GLA)) for s in range(nblk)], axis=0)
        e = jnp.exp(-jnp.abs(b - bref))
        qe = (q * e).astype(BF16)
        ke = k * e
        mask = ((r_i // (2 * m)) == (c_i // (2 * m))) & ((r_i % (2 * m)) >= m) & ((c_i % (2 * m)) < m)
        level_terms.append((qe, ke, mask))
        m *= 2

    eb = jnp.exp(b)
    qinter = (q * eb).astype(BF16)
    state = state_ref[...]
    o_inter = _mm(qinter, state.astype(BF16))

    v = v_ref[...]
    vb = v.astype(BF16)
    for hh in range(H_GLA):
        a = jnp.where(diag_mask, _mm(a16.astype(BF16), til_ref[hh]), 0.0)
        for qe, ke, mask in level_terms:
            kh = jnp.where(head_of_lane == hh, ke, 0.0).astype(BF16)
            a = a + jnp.where(mask, _mm_nt(qe, kh), 0.0)
        o_ref[:, hh * DV_GLA:(hh + 1) * DV_GLA] = (
            _mm(a.astype(BF16), vb[:, hh * DV_GLA:(hh + 1) * DV_GLA])
            + o_inter[:, hh * DV_GLA:(hh + 1) * DV_GLA])

    b_last = b_scr[pl.ds(L - 1, 1), :]
    kdec = (k * jnp.exp(b_last - b)).astype(BF16)
    upd = _mm_tn(kdec, vb)
    rr = lax.broadcasted_iota(jnp.int32, (WK_GLA, WV_GLA), 0) // DK_GLA
    cc = lax.broadcasted_iota(jnp.int32, (WK_GLA, WV_GLA), 1) // DV_GLA
    dec_rows = jnp.broadcast_to(jnp.exp(b_last), (WK_GLA, WK_GLA))
    eye = lax.broadcasted_iota(jnp.int32, (WK_GLA, WK_GLA), 0) == lax.broadcasted_iota(jnp.int32, (WK_GLA, WK_GLA), 1)
    dec_col = jnp.sum(jnp.where(eye, dec_rows, 0.0), axis=1, keepdims=True)
    new_state = jnp.where(rr == cc, dec_col * state + upd, 0.0)
    state_ref[...] = new_state

    @pl.when(c == pl.num_programs(1) - 1)
    def _():
        for hh in range(H_GLA):
            sfin_ref[hh] = new_state[hh * DK_GLA:(hh + 1) * DK_GLA, hh * DV_GLA:(hh + 1) * DV_GLA]


def _gla_consts():
    L = GLA_CHUNK
    hk = jnp.arange(WK_GLA) // DK_GLA
    cols = jnp.arange(H_GLA * GLA_SUB)
    j = jnp.arange(GLA_SUB)
    sel = ((hk[None, :, None] == (cols // GLA_SUB)[None, None, :])
           & (j[:, None, None] == (cols % GLA_SUB)[None, None, :])).astype(BF16)
    s = jnp.arange(L)
    hsel = jnp.arange(H_GLA)
    til = ((hsel[:, None, None] == (cols // GLA_SUB)[None, :, None])
           & ((cols % GLA_SUB)[None, :, None] == (s % GLA_SUB)[None, None, :])).astype(BF16)
    return sel, til


def _gla_prompt(q, k, v, la):
    B, S, _ = q.shape
    L = GLA_CHUNK
    sel, til = _gla_consts()
    row = lambda n: pl.BlockSpec((None, L, n), lambda b, c: (b, c, 0))
    return pl.pallas_call(
        _gla_prompt_kernel,
        grid=(B, S // L),
        in_specs=[row(WK_GLA), row(WK_GLA), row(WV_GLA), row(WK_GLA),
                  pl.BlockSpec(sel.shape, lambda b, c: (0, 0, 0)),
                  pl.BlockSpec(til.shape, lambda b, c: (0, 0, 0))],
        out_specs=[row(WV_GLA),
                   pl.BlockSpec((None, H_GLA, DK_GLA, DV_GLA), lambda b, c: (b, 0, 0, 0))],
        out_shape=[jax.ShapeDtypeStruct((B, S, WV_GLA), F32),
                   jax.ShapeDtypeStruct((B, H_GLA, DK_GLA, DV_GLA), F32)],
        scratch_shapes=[pltpu.VMEM((L, WK_GLA), F32), pltpu.VMEM((L, WK_GLA), F32),
                        pltpu.VMEM((WK_GLA, WV_GLA), F32)],
        compiler_params=_cparams(("arbitrary", "arbitrary")),
        name="gla_prompt",
    )(q, k, v, la, sel, til)


def _merge_kernel(x_ref, of_ref, og_ref, gf_ref, gg_ref, gate_ref, gain_ref, wo_ref, lng_ref, lnb_ref, y_ref):
    x = x_ref[...]
    mix_f = of_ref[...] * _silu(gf_ref[...])
    og = og_ref[...]
    gg = gg_ref[...]
    gain = gain_ref[...]
    parts = [mix_f.astype(BF16)]
    for hh in range(H_GLA):
        o = og[:, hh * DV_GLA:(hh + 1) * DV_GLA]
        o = o * lax.rsqrt(jnp.mean(o * o, axis=-1, keepdims=True) + EPS) * gain
        parts.append((o * _silu(gg[:, hh * DV_GLA:(hh + 1) * DV_GLA])).astype(BF16))
    mix = jnp.concatenate(parts, axis=-1)
    z = DEEPNORM_ALPHA * x + gate_ref[...] * _mm(mix, wo_ref[...])
    mu = jnp.mean(z, axis=-1, keepdims=True)
    zc = z - mu
    var = jnp.mean(zc * zc, axis=-1, keepdims=True)
    y_ref[...] = zc * lax.rsqrt(var + EPS) * lng_ref[...] + lnb_ref[...]


def _merge(l, x, of, og, gf, gg, gate, gain, wo, lng, lnb, tm, per_row_gate):
    B, S, _ = x.shape
    row = lambda n: pl.BlockSpec((None, tm, n), lambda b, i: (b, i, 0))
    gate_spec = row(D_MODEL) if per_row_gate else pl.BlockSpec((None, 1, D_MODEL), lambda b, i: (b, 0, 0))
    vec = lambda n: pl.BlockSpec((None, 1, n), lambda b, i: (l, 0, 0))
    return pl.pallas_call(
        _merge_kernel,
        grid=(B, S // tm),
        in_specs=[row(D_MODEL), row(W_FOX), row(WV_GLA), row(W_FOX), row(WV_GLA), gate_spec,
                  vec(DV_GLA), pl.BlockSpec((None, D_MODEL, D_MODEL), lambda b, i: (l, 0, 0)),
                  vec(D_MODEL), vec(D_MODEL)],
        out_specs=row(D_MODEL),
        out_shape=jax.ShapeDtypeStruct((B, S, D_MODEL), F32),
        compiler_params=_cparams(("arbitrary", "arbitrary")),
        name="merge",
    )(x, of, og, gf, gg, gate, gain, wo, lng, lnb)


_ST_QF, _ST_KF, _ST_VF, _ST_QG, _ST_KG, _ST_GL, _ST_FL = 0, 512, 1024, 1536, 1792, 2048, 2064
_ST_ROWS = 2080
_SR_GF, _SR_VG, _SR_GG = 0, 512, 1024
_SR_COLS = 1536


def _inproj_sample_kernel(x_ref, shift_ref, scale_ref, wr_ref, wt_ref, bf_ref, wgkT_ref, bgk_ref,
                          qfT_ref, kfT_ref, vfT_ref, logfT_ref, qgT_ref, kgT_ref, laT_ref,
                          gf_ref, vg_ref, gg_ref):
    h = _modulated_ln(x_ref[...], shift_ref[...], scale_ref[...]).astype(BF16)
    r = _mm(h, wr_ref[...])
    t = _mm_nt(wt_ref[...], h)
    qfT_ref[...] = t[_ST_QF:_ST_QF + W_FOX]
    kfT_ref[...] = t[_ST_KF:_ST_KF + W_FOX]
    vfT_ref[...] = t[_ST_VF:_ST_VF + W_FOX]
    qgT_ref[...] = t[_ST_QG:_ST_QG + WK_GLA]
    kgT_ref[...] = t[_ST_KG:_ST_KG + WK_GLA]
    logfT = _log_sigmoid(t[_ST_FL:_ST_FL + SUB_BF16] + bf_ref[...])
    logfT_ref[...] = logfT[0:H_FOX]
    glT = t[_ST_GL:_ST_GL + GATE_RANK].astype(BF16)
    laT_ref[...] = _log_sigmoid(_mm(wgkT_ref[...], glT) + bgk_ref[...]) * (1.0 / GATE_NORM)
    gf_ref[...] = r[:, _SR_GF:_SR_GF + W_FOX]
    vg_ref[...] = r[:, _SR_VG:_SR_VG + WV_GLA]
    gg_ref[...] = r[:, _SR_GG:_SR_GG + WV_GLA]


def _inproj_sample(l, x, shift, scale, wr, wt, bf_col, wgkT, bgk_col):
    n = x.shape[0]
    full = lambda a, b: pl.BlockSpec((a, b), lambda g: (0, 0))
    lay = lambda a, b: pl.BlockSpec((None, a, b), lambda g: (l, 0, 0))
    f = lambda a, b: jax.ShapeDtypeStruct((a, b), F32)
    return pl.pallas_call(
        _inproj_sample_kernel,
        grid=(1,),
        in_specs=[full(n, D_MODEL), full(n, D_MODEL), full(n, D_MODEL),
                  lay(D_MODEL, _SR_COLS), lay(_ST_ROWS, D_MODEL), lay(SUB_BF16, 1),
                  lay(WK_GLA, GATE_RANK), lay(WK_GLA, 1)],
        out_specs=[full(W_FOX, n), full(W_FOX, n), full(W_FOX, n), full(H_FOX, n),
                   full(WK_GLA, n), full(WK_GLA, n), full(WK_GLA, n),
                   full(n, W_FOX), full(n, WV_GLA), full(n, WV_GLA)],
        out_shape=[f(W_FOX, n), f(W_FOX, n), f(W_FOX, n), f(H_FOX, n),
                   f(WK_GLA, n), f(WK_GLA, n), f(WK_GLA, n),
                   f(n, W_FOX), f(n, WV_GLA), f(n, WV_GLA)],
        compiler_params=_cparams(("arbitrary",)),
        name="inproj_sample",
    )(x, shift, scale, wr, wt, bf_col, wgkT, bgk_col)


def _fox_sample_kernel(pt_ref, qT_ref, kT_ref, vT_ref, lfT_ref, *refs):
    n_pages = (len(refs) - 1) // 3
    k_pages = refs[0:n_pages]
    v_pages = refs[n_pages:2 * n_pages]
    f_pages = refs[2 * n_pages:3 * n_pages]
    oT_ref = refs[3 * n_pages]
    b = pl.program_id(0)
    nb = qT_ref.shape[1]
    P = PAGE_SIZE

    onehot = lax.broadcasted_iota(jnp.int32, (1, nb), 1) == b
    pick = lambda ref: jnp.sum(jnp.where(onehot, ref[...], 0.0), axis=1, keepdims=True)
    q_col = pick(qT_ref)
    k_col = pick(kT_ref)
    v_col = pick(vT_ref)
    lf_new = pick(lfT_ref)

    r_i = lax.broadcasted_iota(jnp.int32, (P, P), 0)
    c_i = lax.broadcasted_iota(jnp.int32, (P, P), 1)
    upper = jnp.where(r_i <= c_i, 1.0, 0.0).astype(BF16)
    sub8 = lax.broadcasted_iota(jnp.int32, (H_FOX, P), 0)

    q_b = [jnp.broadcast_to(q_col[hh * HD_FOX:(hh + 1) * HD_FOX], (HD_FOX, P)) for hh in range(H_FOX)]

    m = jnp.full((H_FOX, 1), -jnp.inf, F32)
    l = jnp.zeros((H_FOX, 1), F32)
    accs = [jnp.zeros((HD_FOX, P), F32) for _ in range(H_FOX)]
    carry = jnp.zeros((H_FOX, 1), F32)
    for g in range(n_pages):
        lf = f_pages[g][...]
        hi, mid, lo = _split3(lf)
        cum = _mm(hi, upper) + _mm(mid, upper) + _mm(lo, upper) + carry
        carry = carry + jnp.sum(lf, axis=1, keepdims=True)
        s = jnp.zeros((H_FOX, P), F32)
        for hh in range(H_FOX):
            sh = jnp.sum(k_pages[g][hh] * q_b[hh], axis=0, keepdims=True)
            s = jnp.where(sub8 == hh, sh, s)
        x = s - cum
        m_new = jnp.maximum(m, jnp.max(x, axis=1, keepdims=True))
        alpha = jnp.exp(m - m_new)
        p = jnp.exp(x - m_new)
        l = alpha * l + jnp.sum(p, axis=1, keepdims=True)
        m = m_new
        for hh in range(H_FOX):
            accs[hh] = accs[hh] * alpha[hh:hh + 1] + v_pages[g][hh] * p[hh:hh + 1]
    qk = q_col * k_col
    s_self = jnp.concatenate(
        [jnp.sum(qk[hh * HD_FOX:(hh + 1) * HD_FOX], axis=0, keepdims=True) for hh in range(H_FOX)], axis=0)
    x_self = s_self - (carry + lf_new)
    m_new = jnp.maximum(m, x_self)
    alpha = jnp.exp(m - m_new)
    p_self = jnp.exp(x_self - m_new)
    l = alpha * l + p_self
    outs = []
    for hh in range(H_FOX):
        o = jnp.sum(accs[hh], axis=1, keepdims=True) * alpha[hh:hh + 1]
        o = o + p_self[hh:hh + 1] * v_col[hh * HD_FOX:(hh + 1) * HD_FOX]
        outs.append(o * (1.0 / l[hh:hh + 1]))
    o_col = jnp.concatenate(outs, axis=0)

    @pl.when(b == 0)
    def _():
        oT_ref[...] = jnp.zeros_like(oT_ref)
    oT_ref[...] = jnp.where(onehot, o_col, oT_ref[...])


def _fox_sample(l, page_table, qT, kT, vT, lfT, ck, cv, cf):
    nb, n_pages = page_table.shape
    small = lambda a: pl.BlockSpec((a, nb), lambda b, pt: (0, 0))
    kv_spec = lambda g: pl.BlockSpec((None, None, H_FOX, HD_FOX, PAGE_SIZE),
                                     lambda b, pt: (l, pt[b, g], 0, 0, 0))
    f_spec = lambda g: pl.BlockSpec((None, None, H_FOX, PAGE_SIZE), lambda b, pt: (l, pt[b, g], 0, 0))
    in_specs = ([small(W_FOX), small(W_FOX), small(W_FOX), small(H_FOX)]
                + [kv_spec(g) for g in range(n_pages)]
                + [kv_spec(g) for g in range(n_pages)]
                + [f_spec(g) for g in range(n_pages)])
    grid_spec = pltpu.PrefetchScalarGridSpec(
        num_scalar_prefetch=1, grid=(nb,), in_specs=in_specs,
        out_specs=pl.BlockSpec((W_FOX, nb), lambda b, pt: (0, 0)))
    return pl.pallas_call(
        _fox_sample_kernel,
        grid_spec=grid_spec,
        out_shape=jax.ShapeDtypeStruct((W_FOX, nb), F32),
        compiler_params=_cparams(("arbitrary",)),
        name="fox_sample",
    )(page_table, qT, kT, vT, lfT, *([ck] * n_pages), *([cv] * n_pages), *([cf] * n_pages))


def _gla_sample_kernel(qT_ref, kT_ref, laT_ref, v_ref, s_ref, snew_ref, o_ref):
    b = pl.program_id(0)
    nb = qT_ref.shape[1]
    onehot = lax.broadcasted_iota(jnp.int32, (1, nb), 1) == b
    pick = lambda ref: jnp.sum(jnp.where(onehot, ref[...], 0.0), axis=1, keepdims=True)
    q_col = pick(qT_ref)
    k_col = pick(kT_ref)
    a_col = jnp.exp(pick(laT_ref))
    v = v_ref[...]
    for hh in range(H_GLA):
        rows = slice(hh * DK_GLA, (hh + 1) * DK_GLA)
        st = a_col[rows] * s_ref[hh] + k_col[rows] * v[:, hh * DV_GLA:(hh + 1) * DV_GLA]
        snew_ref[hh] = st
        o_ref[:, hh * DV_GLA:(hh + 1) * DV_GLA] = jnp.sum(q_col[rows] * st, axis=0, keepdims=True)


def _gla_sample(l, qT, kT, laT, v, state):
    nb = qT.shape[1]
    small = pl.BlockSpec((WK_GLA, nb), lambda b: (0, 0))
    return pl.pallas_call(
        _gla_sample_kernel,
        grid=(nb,),
        in_specs=[small, small, small,
                  pl.BlockSpec((None, 1, WV_GLA), lambda b: (b, 0, 0)),
                  pl.BlockSpec((None, None, H_GLA, DK_GLA, DV_GLA), lambda b: (l, b, 0, 0, 0))],
        out_specs=[pl.BlockSpec((None, H_GLA, DK_GLA, DV_GLA), lambda b: (b, 0, 0, 0)),
                   pl.BlockSpec((None, 1, WV_GLA), lambda b: (b, 0, 0))],
        out_shape=[jax.ShapeDtypeStruct((nb, H_GLA, DK_GLA, DV_GLA), F32),
                   jax.ShapeDtypeStruct((nb, 1, WV_GLA), F32)],
        compiler_params=_cparams(("arbitrary",)),
        name="gla_sample",
    )(qT, kT, laT, v.reshape(nb, 1, WV_GLA), state)


def _prep_weights(w_in, b_f, w_gk, b_gk):
    sl = lambda off, n: w_in[:, :, off:off + n]
    qf = sl(_QF, W_FOX) * FOX_SCALE
    qg = sl(_QG, WK_GLA) * GLA_SCALE
    kf, vf, fl, gf = sl(_KF, W_FOX), sl(_VF, W_FOX), sl(_FL, H_FOX), sl(_GF, W_FOX)
    kg, vg, gl, gg = sl(_KG, WK_GLA), sl(_VG, WV_GLA), sl(_GL, GATE_RANK), sl(_GG, WV_GLA)
    zc = lambda n: jnp.zeros((DEPTH, D_MODEL, n), F32)
    tr = lambda a: jnp.transpose(a, (0, 2, 1))
    wr_p = jnp.concatenate([qf, gf, qg, kg, vg, gg, gl, zc(LANES - GATE_RANK)], axis=2).astype(BF16)
    fl_pad = jnp.concatenate([fl, zc(SUB_BF16 - H_FOX)], axis=2)
    wt_p = tr(jnp.concatenate([kf, vf, fl_pad], axis=2)).astype(BF16)
    wr_s = jnp.concatenate([gf, vg, gg], axis=2).astype(BF16)
    wt_s = tr(jnp.concatenate([qf, kf, vf, qg, kg, gl, fl_pad], axis=2)).astype(BF16)
    bf_col = jnp.concatenate([b_f, jnp.zeros((DEPTH, SUB_BF16 - H_FOX), F32)], axis=1)[:, :, None]
    wgk_pad = jnp.concatenate([w_gk, jnp.zeros((DEPTH, LANES - GATE_RANK, WK_GLA), F32)], axis=1).astype(BF16)
    wgkT = tr(w_gk).astype(BF16)
    return dict(wr_p=wr_p, wt_p=wt_p, wr_s=wr_s, wt_s=wt_s, bf_col=bf_col, wgk_pad=wgk_pad, wgkT=wgkT,
                bgk_row=b_gk[:, None, :], bgk_col=b_gk[:, :, None])


def kernel(x_prompt, x_sample, cache_k, cache_v, cache_logf, state_gla, page_table, c_prompt, c_sample,
           w_ada, b_ada, w_in, b_f, w_gk, b_gk, gla_gain, w_out, ln_g, ln_b):
    B, S, _ = x_prompt.shape
    nb = x_sample.shape[0]
    W = _prep_weights(w_in, b_f, w_gk, b_gk)
    wo = w_out.astype(BF16)
    gain = gla_gain[:, None, :]
    lng = ln_g[:, None, :]
    lnb = ln_b[:, None, :]

    mod = _ada_all(jnp.concatenate([c_prompt, c_sample], axis=0), w_ada, b_ada)
    ck = jnp.transpose(cache_k, (0, 1, 3, 4, 2))
    cv = jnp.transpose(cache_v, (0, 1, 3, 4, 2))
    cf = jnp.transpose(cache_logf, (0, 1, 3, 2))

    xp = x_prompt
    xs = x_sample.reshape(1, nb, D_MODEL)
    kp_l, vp_l, fp_l, sp_l, ks_l, vs_l, fs_l, ss_l = ([] for _ in range(8))
    for l in range(DEPTH):
        shift_p = mod[l, :B, None, 0:D_MODEL]
        scale_p = mod[l, :B, None, D_MODEL:2 * D_MODEL]
        gate_p = mod[l, :B, None, 2 * D_MODEL:]
        shift_s = mod[l, B:, 0:D_MODEL]
        scale_s = mod[l, B:, D_MODEL:2 * D_MODEL]
        gate_s = mod[l, B:, 2 * D_MODEL:][None]

        (qa, kT, vT, ka, vTb, logfT, gf, qg, kg, vg, la, gg) = _inproj_prompt(
            l, xp, shift_p, scale_p, W["wr_p"], W["wt_p"], W["bf_col"], W["wgk_pad"], W["bgk_row"])
        o_f = _fox_prompt(qa, ka, vTb)
        o_g, s_p = _gla_prompt(qg, kg, vg, la)
        xp = _merge(l, xp, o_f, o_g, gf, gg, gate_p, gain, wo, lng, lnb, tm=256, per_row_gate=False)
        kp_l.append(kT); vp_l.append(vT); fp_l.append(logfT); sp_l.append(s_p)

        (qfT, kfT, vfT, lfT, qgT, kgT, laT, gfs, vgs, ggs) = _inproj_sample(
            l, xs[0], shift_s, scale_s, W["wr_s"], W["wt_s"], W["bf_col"], W["wgkT"], W["bgk_col"])
        o_fT = _fox_sample(l, page_table, qfT, kfT, vfT, lfT, ck, cv, cf)
        s_s, o_gs = _gla_sample(l, qgT, kgT, laT, vgs, state_gla)
        xs = _merge(l, xs, o_fT.T[None], o_gs.reshape(1, nb, WV_GLA), gfs[None], ggs[None], gate_s,
                    gain, wo, lng, lnb, tm=nb, per_row_gate=True)
        ks_l.append(kfT); vs_l.append(vfT); fs_l.append(lfT); ss_l.append(s_s)

    kp = jnp.transpose(jnp.stack(kp_l).reshape(DEPTH, B, H_FOX, HD_FOX, S), (0, 1, 4, 2, 3))
    vp = jnp.transpose(jnp.stack(vp_l).reshape(DEPTH, B, H_FOX, HD_FOX, S), (0, 1, 4, 2, 3))
    fp = jnp.transpose(jnp.stack(fp_l), (0, 1, 3, 2))
    ks = jnp.transpose(jnp.stack(ks_l).reshape(DEPTH, H_FOX, HD_FOX, nb), (0, 3, 1, 2))[:, :, None]
    vs = jnp.transpose(jnp.stack(vs_l).reshape(DEPTH, H_FOX, HD_FOX, nb), (0, 3, 1, 2))[:, :, None]
    fs = jnp.transpose(jnp.stack(fs_l), (0, 2, 1))[:, :, None]
    return (xp, xs.reshape(nb, 1, D_MODEL), kp, vp, fp, jnp.stack(sp_l), ks, vs, fs, jnp.stack(ss_l))
```

```python
import jax
import jax.numpy as jnp
import numpy as np
from jax import lax
from jax.experimental import pallas as pl
from jax.experimental.pallas import tpu as pltpu

D_MODEL = 1024
DEPTH = 4
PAGE_SIZE = 128
HD_FOX = 64
W_FOX = D_MODEL // 2
H_FOX = W_FOX // HD_FOX
H_GLA = 4
DV_GLA = (D_MODEL // 2) // H_GLA
DK_GLA = DV_GLA // 2
WK_GLA = H_GLA * DK_GLA
WV_GLA = H_GLA * DV_GLA
GATE_RANK = 16
GATE_NORM = 16.0
EPS = 1e-5
DEEPNORM_ALPHA = (2.0 * DEPTH) ** 0.25
_SPLITS = (W_FOX, W_FOX, W_FOX, H_FOX, W_FOX, WK_GLA, WK_GLA, WV_GLA, GATE_RANK, WV_GLA)
_OFF = [0]
for _s in _SPLITS:
    _OFF.append(_OFF[-1] + _s)
(_QF, _KF, _VF, _FL, _GF, _QG, _KG, _VG, _GL, _GG) = _OFF[:-1]

LANES = 128
SUB_BF16 = 16
FOX_SCALE = HD_FOX ** -0.5
GLA_SCALE = DK_GLA ** -0.5
GLA_SUB = 8
GLA_CHUNK = 128
FOX_TILE = 512
VMEM_LIMIT = 56 * 1024 * 1024

F32 = jnp.float32
BF16 = jnp.bfloat16


def _cparams(sem):
    return pltpu.CompilerParams(dimension_semantics=sem, vmem_limit_bytes=VMEM_LIMIT)


def _split3(x):
    hi = x.astype(BF16).astype(F32)
    r = x - hi
    mid = r.astype(BF16).astype(F32)
    lo = (r - mid).astype(BF16).astype(F32)
    return hi, mid, lo


def _exact_prefix_mm(tri, x, left):
    hi, mid, lo = _split3(x)
    if left:
        return _mm(tri, hi.astype(BF16)) + _mm(tri, mid.astype(BF16)) + _mm(tri, lo.astype(BF16))
    return _mm(hi.astype(BF16), tri) + _mm(mid.astype(BF16), tri) + _mm(lo.astype(BF16), tri)


def _log_sigmoid(x):
    return -(jnp.maximum(-x, 0.0) + jnp.log1p(jnp.exp(-jnp.abs(x))))


def _silu(x):
    return x * (1.0 / (1.0 + jnp.exp(-x)))


def _mm(a, b):
    return jnp.dot(a, b, preferred_element_type=F32)


def _mm_nt(a, b):
    return lax.dot_general(a, b, (((1,), (1,)), ((), ())), preferred_element_type=F32)


def _mm_tn(a, b):
    return lax.dot_general(a, b, (((0,), (0,)), ((), ())), preferred_element_type=F32)


def _ada_kernel(c_ref, w_ref, b_ref, o_ref):
    c = c_ref[...]
    o_ref[...] = _mm(_silu(c).astype(BF16), w_ref[...].astype(BF16)) + b_ref[...]


def _ada_all(c_all, w_ada, b_ada):
    n = c_all.shape[0]
    return pl.pallas_call(
        _ada_kernel,
        grid=(DEPTH, 3),
        in_specs=[pl.BlockSpec((n, D_MODEL), lambda l, j: (0, 0)),
                  pl.BlockSpec((None, D_MODEL, D_MODEL), lambda l, j: (l, 0, j)),
                  pl.BlockSpec((None, 1, D_MODEL), lambda l, j: (l, 0, j))],
        out_specs=pl.BlockSpec((None, n, D_MODEL), lambda l, j: (l, 0, j)),
        out_shape=jax.ShapeDtypeStruct((DEPTH, n, 3 * D_MODEL), F32),
        compiler_params=_cparams(("arbitrary", "arbitrary")),
        name="ada_mod",
    )(c_all, w_ada, b_ada.reshape(DEPTH, 1, 3 * D_MODEL))


def _modulated_ln(x, shift, scale):
    mu = jnp.mean(x, axis=-1, keepdims=True)
    xc = x - mu
    var = jnp.mean(xc * xc, axis=-1, keepdims=True)
    return (xc * lax.rsqrt(var + EPS)) * (1.0 + scale) + shift


_R_KF, _R_QG, _R_KG, _R_VG, _R_GG, _R_GLFL = 0, 512, 768, 1024, 1536, 2048
_R_COLS = 2176
_FL_LANE = GATE_RANK
_T_QF, _T_KF, _T_VF, _T_GF, _T_FL = 0, 512, 1024, 1536, 2048
_T_ROWS = 2048 + SUB_BF16


def _inproj_prompt_kernel(x_ref, shift_ref, scale_ref, wr_ref, wt_ref, bfc_ref, bfr_ref, wgk_ref, bgk_ref,
                          qT_ref, ka_ref, vTb_ref, kT_ref, vT_ref, logfT_ref, gfT_ref,
                          qg_ref, kg_ref, vg_ref, la_ref, gg_ref, carry_ref):
    i = pl.program_id(1)
    tm = x_ref.shape[0]
    h = _modulated_ln(x_ref[...], shift_ref[...], scale_ref[...]).astype(BF16)
    r = _mm(h, wr_ref[...])
    t = _mm_nt(wt_ref[...], h)

    kT = t[_T_KF:_T_KF + W_FOX]
    vT = t[_T_VF:_T_VF + W_FOX]
    kT_ref[...] = kT
    vT_ref[...] = vT
    vTb_ref[...] = vT.astype(BF16)
    gfT_ref[...] = t[_T_GF:_T_GF + W_FOX]
    logfT_ref[...] = _log_sigmoid(t[_T_FL:_T_FL + SUB_BF16] + bfc_ref[...])[0:H_FOX]

    sub = lax.broadcasted_iota(jnp.int32, (SUB_BF16, tm), 0)
    ones3 = jnp.where(sub < 3, 1.0, 0.0).astype(BF16)
    zeros = jnp.zeros((LANES - HD_FOX - SUB_BF16, tm), BF16)
    for hh in range(H_FOX):
        qT_ref[hh, 0:HD_FOX, :] = t[_T_QF + hh * HD_FOX:_T_QF + (hh + 1) * HD_FOX].astype(BF16)
        qT_ref[hh, HD_FOX:HD_FOX + SUB_BF16, :] = ones3
        qT_ref[hh, HD_FOX + SUB_BF16:LANES, :] = zeros

    @pl.when(i == 0)
    def _():
        carry_ref[...] = jnp.zeros_like(carry_ref)

    glfl = r[:, _R_GLFL:_R_GLFL + LANES]
    logf_r = _log_sigmoid(glfl + bfr_ref[...])
    row = lax.broadcasted_iota(jnp.int32, (tm, tm), 0)
    col = lax.broadcasted_iota(jnp.int32, (tm, tm), 1)
    lower = jnp.where(col <= row, 1.0, 0.0).astype(BF16)
    carry = carry_ref[0:1, :]
    cum = _exact_prefix_mm(lower, logf_r, left=True) + carry
    carry_ref[...] = jnp.broadcast_to(carry + jnp.sum(logf_r, axis=0, keepdims=True), carry_ref.shape)
    chi, cmid, clo = _split3(cum)

    lane = lax.broadcasted_iota(jnp.int32, (tm, LANES), 1)
    for hh in range(H_FOX):
        blk = r[:, _R_KF + (hh // 2) * LANES:_R_KF + (hh // 2 + 1) * LANES]
        if hh % 2:
            blk = pltpu.roll(blk, HD_FOX, axis=1)
        src = _FL_LANE + hh
        e0 = pltpu.roll(chi, HD_FOX - src, axis=1)
        e1 = pltpu.roll(cmid, HD_FOX + 1 - src, axis=1)
        e2 = pltpu.roll(clo, HD_FOX + 2 - src, axis=1)
        extra = jnp.where(lane == HD_FOX, -e0,
                          jnp.where(lane == HD_FOX + 1, -e1, jnp.where(lane == HD_FOX + 2, -e2, 0.0)))
        ka_ref[hh] = jnp.where(lane < HD_FOX, blk, extra).astype(BF16)

    qg_ref[...] = r[:, _R_QG:_R_QG + WK_GLA]
    kg_ref[...] = r[:, _R_KG:_R_KG + WK_GLA]
    vg_ref[...] = r[:, _R_VG:_R_VG + WV_GLA]
    gg_ref[...] = r[:, _R_GG:_R_GG + WV_GLA]
    la_ref[...] = _log_sigmoid(_mm(glfl.astype(BF16), wgk_ref[...]) + bgk_ref[...]) * (1.0 / GATE_NORM)


def _inproj_prompt(l, x, shift, scale, W, tm=256):
    B, S, _ = x.shape
    nt = S // tm
    row = lambda n: pl.BlockSpec((None, tm, n), lambda b, i: (b, i, 0))
    colT = lambda n: pl.BlockSpec((None, n, tm), lambda b, i: (b, 0, i))
    lay = lambda a, c: pl.BlockSpec((None, a, c), lambda b, i: (l, 0, 0))
    out_shape = [
        jax.ShapeDtypeStruct((B, H_FOX, LANES, S), BF16),
        jax.ShapeDtypeStruct((B, H_FOX, S, LANES), BF16),
        jax.ShapeDtypeStruct((B, W_FOX, S), BF16),
        jax.ShapeDtypeStruct((B, W_FOX, S), F32),
        jax.ShapeDtypeStruct((B, W_FOX, S), F32),
        jax.ShapeDtypeStruct((B, H_FOX, S), F32),
        jax.ShapeDtypeStruct((B, W_FOX, S), F32),
        jax.ShapeDtypeStruct((B, S, WK_GLA), F32),
        jax.ShapeDtypeStruct((B, S, WK_GLA), F32),
        jax.ShapeDtypeStruct((B, S, WV_GLA), F32),
        jax.ShapeDtypeStruct((B, S, WK_GLA), F32),
        jax.ShapeDtypeStruct((B, S, WV_GLA), F32),
    ]
    out_specs = [
        pl.BlockSpec((None, H_FOX, LANES, tm), lambda b, i: (b, 0, 0, i)),
        pl.BlockSpec((None, H_FOX, tm, LANES), lambda b, i: (b, 0, i, 0)),
        colT(W_FOX), colT(W_FOX), colT(W_FOX), colT(H_FOX), colT(W_FOX),
        row(WK_GLA), row(WK_GLA), row(WV_GLA), row(WK_GLA), row(WV_GLA),
    ]
    in_specs = [
        pl.BlockSpec((None, tm, D_MODEL), lambda b, i: (b, i, 0)),
        pl.BlockSpec((None, 1, D_MODEL), lambda b, i: (b, 0, 0)),
        pl.BlockSpec((None, 1, D_MODEL), lambda b, i: (b, 0, 0)),
        lay(D_MODEL, _R_COLS), lay(_T_ROWS, D_MODEL), lay(SUB_BF16, 1), lay(1, LANES),
        lay(LANES, WK_GLA), lay(1, WK_GLA),
    ]
    return pl.pallas_call(
        _inproj_prompt_kernel,
        grid=(B, nt),
        in_specs=in_specs,
        out_specs=out_specs,
        out_shape=out_shape,
        scratch_shapes=[pltpu.VMEM((8, LANES), F32)],
        compiler_params=_cparams(("arbitrary", "arbitrary")),
        name="inproj_prompt",
    )(x, shift, scale, W["wr_p"], W["wt_p"], W["bf_col"], W["bf_row"], W["wgk_pad"], W["bgk_row"])


def _fox_prompt_kernel(it_ref, jt_ref, ka_ref, qT_ref, vT_ref, oT_ref, m_ref, l_ref, acc_ref):
    pidx = pl.program_id(1)
    i = it_ref[pidx]
    j = jt_ref[pidx]
    tk = ka_ref.shape[1]
    tq = qT_ref.shape[2]

    @pl.when(j == 0)
    def _():
        m_ref[...] = jnp.full_like(m_ref, -jnp.inf)
        l_ref[...] = jnp.zeros_like(l_ref)
        acc_ref[...] = jnp.zeros_like(acc_ref)

    def run(masked):
        def body(hp, carry):
            for u in range(2):
                hh = 2 * hp + u
                sT = _mm(ka_ref[hh], qT_ref[hh])
                if masked:
                    kpos = lax.broadcasted_iota(jnp.int32, (tk, tq), 0)
                    qpos = lax.broadcasted_iota(jnp.int32, (tk, tq), 1)
                    sT = jnp.where(kpos <= qpos, sT, -jnp.inf)
                m_prev = m_ref[hh]
                m_new = jnp.maximum(m_prev, jnp.max(sT, axis=0, keepdims=True))
                alpha = jnp.exp(m_prev - m_new)
                pT = jnp.exp(sT - m_new)
                l_ref[hh] = alpha * l_ref[hh] + jnp.sum(pT, axis=0, keepdims=True)
                m_ref[hh] = m_new
                rows = pl.ds(pl.multiple_of(hh * HD_FOX, HD_FOX), HD_FOX)
                acc_ref[rows, :] = acc_ref[rows, :] * alpha + _mm(vT_ref[rows, :], pT.astype(BF16))
            return carry
        lax.fori_loop(0, H_FOX // 2, body, 0)

    @pl.when(j < i)
    def _():
        run(False)

    @pl.when(j == i)
    def _():
        run(True)
        for hh in range(H_FOX):
            rows = slice(hh * HD_FOX, (hh + 1) * HD_FOX)
            oT_ref[rows, :] = acc_ref[rows, :] * (1.0 / l_ref[hh])


def _fox_prompt(qT, ka, vTb):
    B, _, S, _ = ka.shape
    t = min(FOX_TILE, S)
    n = S // t
    pairs = [(i, j) for i in range(n) for j in range(i + 1)]
    it = jnp.asarray(np.array([p[0] for p in pairs], np.int32))
    jt = jnp.asarray(np.array([p[1] for p in pairs], np.int32))
    grid_spec = pltpu.PrefetchScalarGridSpec(
        num_scalar_prefetch=2, grid=(B, len(pairs)),
        in_specs=[pl.BlockSpec((None, H_FOX, t, LANES), lambda b, p, it, jt: (b, 0, jt[p], 0)),
                  pl.BlockSpec((None, H_FOX, LANES, t), lambda b, p, it, jt: (b, 0, 0, it[p])),
                  pl.BlockSpec((None, W_FOX, t), lambda b, p, it, jt: (b, 0, jt[p]))],
        out_specs=pl.BlockSpec((None, W_FOX, t), lambda b, p, it, jt: (b, 0, it[p])),
        scratch_shapes=[pltpu.VMEM((H_FOX, 1, t), F32), pltpu.VMEM((H_FOX, 1, t), F32),
                        pltpu.VMEM((W_FOX, t), F32)])
    return pl.pallas_call(
        _fox_prompt_kernel,
        grid_spec=grid_spec,
        out_shape=jax.ShapeDtypeStruct((B, W_FOX, S), F32),
        compiler_params=_cparams(("arbitrary", "arbitrary")),
        name="fox_prompt",
    )(it, jt, ka, qT, vTb)


def _gla_prompt_kernel(q_ref, k_ref, v_ref, la_ref, sel_ref, til_ref, o_ref, sfin_ref,
                       b_scr, k_scr, state_ref):
    c = pl.program_id(1)
    L = q_ref.shape[0]
    nsub = L // GLA_SUB

    @pl.when(c == 0)
    def _():
        state_ref[...] = jnp.zeros_like(state_ref)

    q = q_ref[...]
    k = k_ref[...]

    r_i = lax.broadcasted_iota(jnp.int32, (L, L), 0)
    c_i = lax.broadcasted_iota(jnp.int32, (L, L), 1)
    lower = jnp.where(c_i <= r_i, 1.0, 0.0).astype(BF16)
    b = _exact_prefix_mm(lower, la_ref[...], left=True)
    b_scr[...] = b
    k_scr[...] = k

    head_of_lane = lax.broadcasted_iota(jnp.int32, (L, WK_GLA), 1) // DK_GLA

    def rows_bcast(ref, first, stride, n, reps):
        return jnp.concatenate(
            [jnp.broadcast_to(ref[first + stride * s:first + stride * s + 1, :], (reps, WK_GLA)) for s in range(n)],
            axis=0)

    a8 = jnp.zeros((L, H_GLA * GLA_SUB), F32)
    for j in range(GLA_SUB):
        bj = rows_bcast(b_scr, j, GLA_SUB, nsub, GLA_SUB)
        kj = rows_bcast(k_scr, j, GLA_SUB, nsub, GLA_SUB)
        p = q * kj * jnp.exp(jnp.minimum(b - bj, 0.0))
        a8 = a8 + _mm(p.astype(BF16), sel_ref[j])
    a_exp = _mm(a8.astype(BF16), til_ref[...])
    diag_mask = ((r_i // GLA_SUB) == (c_i // GLA_SUB)) & ((c_i % GLA_SUB) <= (r_i % GLA_SUB))
    a_heads = [jnp.where(diag_mask, a_exp[:, hh * L:(hh + 1) * L], 0.0) for hh in range(H_GLA)]

    m = GLA_SUB
    while m < L:
        bref = rows_bcast(b_scr, m - 1, 2 * m, L // (2 * m), 2 * m)
        e = jnp.exp(-jnp.abs(b - bref))
        qe = q * e
        ke = (k * e).astype(BF16)
        qs = jnp.concatenate([jnp.where(head_of_lane == hh, qe, 0.0).astype(BF16) for hh in range(H_GLA)], axis=0)
        prod = _mm_nt(qs, ke)
        mask = ((r_i // (2 * m)) == (c_i // (2 * m))) & ((r_i % (2 * m)) >= m) & ((c_i % (2 * m)) < m)
        a_heads = [jnp.where(mask, prod[hh * L:(hh + 1) * L], a_heads[hh]) for hh in range(H_GLA)]
        m *= 2

    state = state_ref[...]
    o_inter = _mm((q * jnp.exp(b)).astype(BF16), state.astype(BF16))
    vb = v_ref[...].astype(BF16)
    for hh in range(H_GLA):
        cols = slice(hh * DV_GLA, (hh + 1) * DV_GLA)
        o_ref[:, cols] = _mm(a_heads[hh].astype(BF16), vb[:, cols]) + o_inter[:, cols]

    b_last = b_scr[L - 1:L, :]
    upd = _mm_tn((k * jnp.exp(b_last - b)).astype(BF16), vb)
    rr = lax.broadcasted_iota(jnp.int32, (WK_GLA, WV_GLA), 0) // DK_GLA
    cc = lax.broadcasted_iota(jnp.int32, (WK_GLA, WV_GLA), 1) // DV_GLA
    eye = (lax.broadcasted_iota(jnp.int32, (WK_GLA, WK_GLA), 0)
           == lax.broadcasted_iota(jnp.int32, (WK_GLA, WK_GLA), 1))
    dec_col = jnp.sum(jnp.where(eye, jnp.broadcast_to(jnp.exp(b_last), (WK_GLA, WK_GLA)), 0.0),
                      axis=1, keepdims=True)
    new_state = jnp.where(rr == cc, dec_col * state + upd, 0.0)
    state_ref[...] = new_state

    @pl.when(c == pl.num_programs(1) - 1)
    def _():
        for hh in range(H_GLA):
            sfin_ref[hh] = new_state[hh * DK_GLA:(hh + 1) * DK_GLA, hh * DV_GLA:(hh + 1) * DV_GLA]


def _gla_consts():
    L = GLA_CHUNK
    n = H_GLA * GLA_SUB
    hk = np.arange(WK_GLA) // DK_GLA
    cols = np.arange(n)
    j = np.arange(GLA_SUB)
    sel = ((hk[None, :, None] == (cols // GLA_SUB)[None, None, :])
           & (j[:, None, None] == (cols % GLA_SUB)[None, None, :]))
    out = np.arange(H_GLA * L)
    til = (((cols // GLA_SUB)[:, None] == (out // L)[None, :])
           & ((cols % GLA_SUB)[:, None] == ((out % L) % GLA_SUB)[None, :]))
    return jnp.asarray(sel, BF16), jnp.asarray(til, BF16)


def _gla_prompt(q, k, v, la):
    B, S, _ = q.shape
    L = GLA_CHUNK
    sel, til = _gla_consts()
    row = lambda n: pl.BlockSpec((None, L, n), lambda b, c: (b, c, 0))
    return pl.pallas_call(
        _gla_prompt_kernel,
        grid=(B, S // L),
        in_specs=[row(WK_GLA), row(WK_GLA), row(WV_GLA), row(WK_GLA),
                  pl.BlockSpec(sel.shape, lambda b, c: (0, 0, 0)),
                  pl.BlockSpec(til.shape, lambda b, c: (0, 0))],
        out_specs=[row(WV_GLA),
                   pl.BlockSpec((None, H_GLA, DK_GLA, DV_GLA), lambda b, c: (b, 0, 0, 0))],
        out_shape=[jax.ShapeDtypeStruct((B, S, WV_GLA), F32),
                   jax.ShapeDtypeStruct((B, H_GLA, DK_GLA, DV_GLA), F32)],
        scratch_shapes=[pltpu.VMEM((L, WK_GLA), F32), pltpu.VMEM((L, WK_GLA), F32),
                        pltpu.VMEM((WK_GLA, WV_GLA), F32)],
        compiler_params=_cparams(("arbitrary", "arbitrary")),
        name="gla_prompt",
    )(q, k, v, la, sel, til)


def _merge_kernel(x_ref, ofT_ref, gfT_ref, og_ref, gg_ref, gate_ref, gain_ref, wo_ref, lng_ref, lnb_ref, y_ref):
    x = x_ref[...]
    mix_fT = (ofT_ref[...] * _silu(gfT_ref[...])).astype(BF16)
    og = og_ref[...]
    gg = gg_ref[...]
    gain = gain_ref[...]
    parts = []
    for hh in range(H_GLA):
        cols = slice(hh * DV_GLA, (hh + 1) * DV_GLA)
        o = og[:, cols]
        o = o * lax.rsqrt(jnp.mean(o * o, axis=-1, keepdims=True) + EPS) * gain
        parts.append((o * _silu(gg[:, cols])).astype(BF16))
    mix_g = jnp.concatenate(parts, axis=-1)
    y = _mm_tn(mix_fT, wo_ref[0:W_FOX, :]) + _mm(mix_g, wo_ref[W_FOX:D_MODEL, :])
    z = DEEPNORM_ALPHA * x + gate_ref[...] * y
    mu = jnp.mean(z, axis=-1, keepdims=True)
    zc = z - mu
    var = jnp.mean(zc * zc, axis=-1, keepdims=True)
    y_ref[...] = zc * lax.rsqrt(var + EPS) * lng_ref[...] + lnb_ref[...]


def _merge(l, x, ofT, gfT, og, gg, gate, gain, wo, lng, lnb, tm, per_row_gate):
    B, S, _ = x.shape
    row = lambda n: pl.BlockSpec((None, tm, n), lambda b, i: (b, i, 0))
    colT = lambda n: pl.BlockSpec((None, n, tm), lambda b, i: (b, 0, i))
    gate_spec = row(D_MODEL) if per_row_gate else pl.BlockSpec((None, 1, D_MODEL), lambda b, i: (b, 0, 0))
    vec = lambda n: pl.BlockSpec((None, 1, n), lambda b, i: (l, 0, 0))
    return pl.pallas_call(
        _merge_kernel,
        grid=(B, S // tm),
        in_specs=[row(D_MODEL), colT(W_FOX), colT(W_FOX), row(WV_GLA), row(WV_GLA), gate_spec,
                  vec(DV_GLA), pl.BlockSpec((None, D_MODEL, D_MODEL), lambda b, i: (l, 0, 0)),
                  vec(D_MODEL), vec(D_MODEL)],
        out_specs=row(D_MODEL),
        out_shape=jax.ShapeDtypeStruct((B, S, D_MODEL), F32),
        compiler_params=_cparams(("arbitrary", "arbitrary")),
        name="merge",
    )(x, ofT, gfT, og, gg, gate, gain, wo, lng, lnb)


_ST_QF, _ST_KF, _ST_VF, _ST_GF, _ST_QG, _ST_KG, _ST_GL, _ST_FL = 0, 512, 1024, 1536, 2048, 2304, 2560, 2576
_ST_ROWS = 2592
_SR_VG, _SR_GG = 0, 512
_SR_COLS = 1024


def _inproj_sample_kernel(x_ref, shift_ref, scale_ref, wr_ref, wt_ref, bf_ref, wgkT_ref, bgk_ref,
                          qfT_ref, kfT_ref, vfT_ref, gfT_ref, logfT_ref, qgT_ref, kgT_ref, laT_ref,
                          vg_ref, gg_ref):
    h = _modulated_ln(x_ref[...], shift_ref[...], scale_ref[...]).astype(BF16)
    r = _mm(h, wr_ref[...])
    t = _mm_nt(wt_ref[...], h)
    qfT_ref[...] = t[_ST_QF:_ST_QF + W_FOX]
    kfT_ref[...] = t[_ST_KF:_ST_KF + W_FOX]
    vfT_ref[...] = t[_ST_VF:_ST_VF + W_FOX]
    gfT_ref[...] = t[_ST_GF:_ST_GF + W_FOX]
    qgT_ref[...] = t[_ST_QG:_ST_QG + WK_GLA]
    kgT_ref[...] = t[_ST_KG:_ST_KG + WK_GLA]
    logfT_ref[...] = _log_sigmoid(t[_ST_FL:_ST_FL + SUB_BF16] + bf_ref[...])[0:H_FOX]
    glT = t[_ST_GL:_ST_GL + GATE_RANK].astype(BF16)
    laT_ref[...] = _log_sigmoid(_mm(wgkT_ref[...], glT) + bgk_ref[...]) * (1.0 / GATE_NORM)
    vg_ref[...] = r[:, _SR_VG:_SR_VG + WV_GLA]
    gg_ref[...] = r[:, _SR_GG:_SR_GG + WV_GLA]


def _inproj_sample(l, x, shift, scale, W):
    n = x.shape[0]
    full = lambda a, b: pl.BlockSpec((a, b), lambda g: (0, 0))
    lay = lambda a, b: pl.BlockSpec((None, a, b), lambda g: (l, 0, 0))
    f = lambda a, b: jax.ShapeDtypeStruct((a, b), F32)
    return pl.pallas_call(
        _inproj_sample_kernel,
        grid=(1,),
        in_specs=[full(n, D_MODEL), full(n, D_MODEL), full(n, D_MODEL),
                  lay(D_MODEL, _SR_COLS), lay(_ST_ROWS, D_MODEL), lay(SUB_BF16, 1),
                  lay(WK_GLA, GATE_RANK), lay(WK_GLA, 1)],
        out_specs=[full(W_FOX, n), full(W_FOX, n), full(W_FOX, n), full(W_FOX, n), full(H_FOX, n),
                   full(WK_GLA, n), full(WK_GLA, n), full(WK_GLA, n),
                   full(n, WV_GLA), full(n, WV_GLA)],
        out_shape=[f(W_FOX, n), f(W_FOX, n), f(W_FOX, n), f(W_FOX, n), f(H_FOX, n),
                   f(WK_GLA, n), f(WK_GLA, n), f(WK_GLA, n),
                   f(n, WV_GLA), f(n, WV_GLA)],
        compiler_params=_cparams(("arbitrary",)),
        name="inproj_sample",
    )(x, shift, scale, W["wr_s"], W["wt_s"], W["bf_col"], W["wgkT"], W["bgk_col"])


def _fox_sample_kernel(pt_ref, qT_ref, kT_ref, vT_ref, lfT_ref, *refs):
    n_pages = (len(refs) - 1) // 3
    k_pages = refs[0:n_pages]
    v_pages = refs[n_pages:2 * n_pages]
    f_pages = refs[2 * n_pages:3 * n_pages]
    oT_ref = refs[3 * n_pages]
    b = pl.program_id(0)
    nb = qT_ref.shape[1]
    P = PAGE_SIZE

    onehot = lax.broadcasted_iota(jnp.int32, (1, nb), 1) == b
    pick = lambda ref: jnp.sum(jnp.where(onehot, ref[...], 0.0), axis=1, keepdims=True)
    q_col = pick(qT_ref)
    k_col = pick(kT_ref)
    v_col = pick(vT_ref)
    lf_new = pick(lfT_ref)

    r_i = lax.broadcasted_iota(jnp.int32, (P, P), 0)
    c_i = lax.broadcasted_iota(jnp.int32, (P, P), 1)
    upper = jnp.where(r_i <= c_i, 1.0, 0.0).astype(BF16)
    sub8 = lax.broadcasted_iota(jnp.int32, (H_FOX, P), 0)

    q_b = [jnp.broadcast_to(q_col[hh * HD_FOX:(hh + 1) * HD_FOX], (HD_FOX, P)) for hh in range(H_FOX)]

    m = jnp.full((H_FOX, 1), -jnp.inf, F32)
    l = jnp.zeros((H_FOX, 1), F32)
    accs = [jnp.zeros((HD_FOX, P), F32) for _ in range(H_FOX)]
    carry = jnp.zeros((H_FOX, 1), F32)
    for g in range(n_pages):
        lf = f_pages[g][...]
        cum = _exact_prefix_mm(upper, lf, left=False) + carry
        carry = carry + jnp.sum(lf, axis=1, keepdims=True)
        s = jnp.zeros((H_FOX, P), F32)
        for hh in range(H_FOX):
            sh = jnp.sum(k_pages[g][hh] * q_b[hh], axis=0, keepdims=True)
            s = jnp.where(sub8 == hh, sh, s)
        x = s - cum
        m_new = jnp.maximum(m, jnp.max(x, axis=1, keepdims=True))
        alpha = jnp.exp(m - m_new)
        p = jnp.exp(x - m_new)
        l = alpha * l + jnp.sum(p, axis=1, keepdims=True)
        m = m_new
        for hh in range(H_FOX):
            accs[hh] = accs[hh] * alpha[hh:hh + 1] + v_pages[g][hh] * p[hh:hh + 1]
    qk = q_col * k_col
    s_self = jnp.concatenate(
        [jnp.sum(qk[hh * HD_FOX:(hh + 1) * HD_FOX], axis=0, keepdims=True) for hh in range(H_FOX)], axis=0)
    x_self = s_self - (carry + lf_new)
    m_new = jnp.maximum(m, x_self)
    alpha = jnp.exp(m - m_new)
    p_self = jnp.exp(x_self - m_new)
    l = alpha * l + p_self
    outs = []
    for hh in range(H_FOX):
        o = jnp.sum(accs[hh], axis=1, keepdims=True) * alpha[hh:hh + 1]
        o = o + p_self[hh:hh + 1] * v_col[hh * HD_FOX:(hh + 1) * HD_FOX]
        outs.append(o * (1.0 / l[hh:hh + 1]))
    o_col = jnp.concatenate(outs, axis=0)

    @pl.when(b == 0)
    def _():
        oT_ref[...] = jnp.zeros_like(oT_ref)
    oT_ref[...] = jnp.where(onehot, o_col, oT_ref[...])


def _fox_sample(l, page_table, qT, kT, vT, lfT, ck, cv, cf):
    nb, n_pages = page_table.shape
    small = lambda a: pl.BlockSpec((a, nb), lambda b, pt: (0, 0))
    kv_spec = lambda g: pl.BlockSpec((None, None, H_FOX, HD_FOX, PAGE_SIZE),
                                     lambda b, pt: (l, pt[b, g], 0, 0, 0))
    f_spec = lambda g: pl.BlockSpec((None, None, H_FOX, PAGE_SIZE), lambda b, pt: (l, pt[b, g], 0, 0))
    in_specs = ([small(W_FOX), small(W_FOX), small(W_FOX), small(H_FOX)]
                + [kv_spec(g) for g in range(n_pages)]
                + [kv_spec(g) for g in range(n_pages)]
                + [f_spec(g) for g in range(n_pages)])
    grid_spec = pltpu.PrefetchScalarGridSpec(
        num_scalar_prefetch=1, grid=(nb,), in_specs=in_specs,
        out_specs=pl.BlockSpec((W_FOX, nb), lambda b, pt: (0, 0)))
    return pl.pallas_call(
        _fox_sample_kernel,
        grid_spec=grid_spec,
        out_shape=jax.ShapeDtypeStruct((W_FOX, nb), F32),
        compiler_params=_cparams(("arbitrary",)),
        name="fox_sample",
    )(page_table, qT, kT, vT, lfT, *([ck] * n_pages), *([cv] * n_pages), *([cf] * n_pages))


def _gla_sample_kernel(qT_ref, kT_ref, laT_ref, v_ref, s_ref, snew_ref, o_ref):
    b = pl.program_id(0)
    nb = qT_ref.shape[1]
    onehot = lax.broadcasted_iota(jnp.int32, (1, nb), 1) == b
    pick = lambda ref: jnp.sum(jnp.where(onehot, ref[...], 0.0), axis=1, keepdims=True)
    q_col = pick(qT_ref)
    k_col = pick(kT_ref)
    a_col = jnp.exp(pick(laT_ref))
    v = v_ref[...]
    for hh in range(H_GLA):
        rows = slice(hh * DK_GLA, (hh + 1) * DK_GLA)
        st = a_col[rows] * s_ref[hh] + k_col[rows] * v[:, hh * DV_GLA:(hh + 1) * DV_GLA]
        snew_ref[hh] = st
        o_ref[:, hh * DV_GLA:(hh + 1) * DV_GLA] = jnp.sum(q_col[rows] * st, axis=0, keepdims=True)


def _gla_sample(l, qT, kT, laT, v, state):
    nb = qT.shape[1]
    small = pl.BlockSpec((WK_GLA, nb), lambda b: (0, 0))
    return pl.pallas_call(
        _gla_sample_kernel,
        grid=(nb,),
        in_specs=[small, small, small,
                  pl.BlockSpec((None, 1, WV_GLA), lambda b: (b, 0, 0)),
                  pl.BlockSpec((None, None, H_GLA, DK_GLA, DV_GLA), lambda b: (l, b, 0, 0, 0))],
        out_specs=[pl.BlockSpec((None, H_GLA, DK_GLA, DV_GLA), lambda b: (b, 0, 0, 0)),
                   pl.BlockSpec((None, 1, WV_GLA), lambda b: (b, 0, 0))],
        out_shape=[jax.ShapeDtypeStruct((nb, H_GLA, DK_GLA, DV_GLA), F32),
                   jax.ShapeDtypeStruct((nb, 1, WV_GLA), F32)],
        compiler_params=_cparams(("arbitrary",)),
        name="gla_sample",
    )(qT, kT, laT, v.reshape(nb, 1, WV_GLA), state)


def _prep_weights(w_in, b_f, w_gk, b_gk):
    sl = lambda off, n: w_in[:, :, off:off + n]
    qf = sl(_QF, W_FOX) * FOX_SCALE
    qg = sl(_QG, WK_GLA) * GLA_SCALE
    kf, vf, fl, gf = sl(_KF, W_FOX), sl(_VF, W_FOX), sl(_FL, H_FOX), sl(_GF, W_FOX)
    kg, vg, gl, gg = sl(_KG, WK_GLA), sl(_VG, WV_GLA), sl(_GL, GATE_RANK), sl(_GG, WV_GLA)
    zc = lambda n: jnp.zeros((DEPTH, D_MODEL, n), F32)
    tr = lambda a: jnp.transpose(a, (0, 2, 1))
    fl_pad = jnp.concatenate([fl, zc(SUB_BF16 - H_FOX)], axis=2)
    glfl = jnp.concatenate([gl, fl, zc(LANES - GATE_RANK - H_FOX)], axis=2)
    wr_p = jnp.concatenate([kf, qg, kg, vg, gg, glfl], axis=2).astype(BF16)
    wt_p = tr(jnp.concatenate([qf, kf, vf, gf, fl_pad], axis=2)).astype(BF16)
    wr_s = jnp.concatenate([vg, gg], axis=2).astype(BF16)
    wt_s = tr(jnp.concatenate([qf, kf, vf, gf, qg, kg, gl, fl_pad], axis=2)).astype(BF16)
    zf = lambda n: jnp.zeros((DEPTH, n), F32)
    bf_col = jnp.concatenate([b_f, zf(SUB_BF16 - H_FOX)], axis=1)[:, :, None]
    bf_row = jnp.concatenate([zf(_FL_LANE), b_f, zf(LANES - _FL_LANE - H_FOX)], axis=1)[:, None, :]
    wgk_pad = jnp.concatenate([w_gk, jnp.zeros((DEPTH, LANES - GATE_RANK, WK_GLA), F32)], axis=1).astype(BF16)
    wgkT = tr(w_gk).astype(BF16)
    return dict(wr_p=wr_p, wt_p=wt_p, wr_s=wr_s, wt_s=wt_s, bf_col=bf_col, bf_row=bf_row,
                wgk_pad=wgk_pad, wgkT=wgkT, bgk_row=b_gk[:, None, :], bgk_col=b_gk[:, :, None])


def kernel(x_prompt, x_sample, cache_k, cache_v, cache_logf, state_gla, page_table, c_prompt, c_sample,
           w_ada, b_ada, w_in, b_f, w_gk, b_gk, gla_gain, w_out, ln_g, ln_b):
    B, S, _ = x_prompt.shape
    nb = x_sample.shape[0]
    W = _prep_weights(w_in, b_f, w_gk, b_gk)
    wo = w_out.astype(BF16)
    gain = gla_gain[:, None, :]
    lng = ln_g[:, None, :]
    lnb = ln_b[:, None, :]

    mod = _ada_all(jnp.concatenate([c_prompt, c_sample], axis=0), w_ada, b_ada)
    ck = jnp.transpose(cache_k, (0, 1, 3, 4, 2))
    cv = jnp.transpose(cache_v, (0, 1, 3, 4, 2))
    cf = jnp.transpose(cache_logf, (0, 1, 3, 2))

    xp = x_prompt
    xs = x_sample.reshape(1, nb, D_MODEL)
    kp_l, vp_l, fp_l, sp_l, ks_l, vs_l, fs_l, ss_l = ([] for _ in range(8))
    for l in range(DEPTH):
        shift_p = mod[l, :B, None, 0:D_MODEL]
        scale_p = mod[l, :B, None, D_MODEL:2 * D_MODEL]
        gate_p = mod[l, :B, None, 2 * D_MODEL:]
        shift_s = mod[l, B:, 0:D_MODEL]
        scale_s = mod[l, B:, D_MODEL:2 * D_MODEL]
        gate_s = mod[l, B:, 2 * D_MODEL:][None]

        (qT, ka, vTb, kT, vT, logfT, gfT, qg, kg, vg, la, gg) = _inproj_prompt(l, xp, shift_p, scale_p, W)
        o_fT = _fox_prompt(qT, ka, vTb)
        o_g, s_p = _gla_prompt(qg, kg, vg, la)
        xp = _merge(l, xp, o_fT, gfT, o_g, gg, gate_p, gain, wo, lng, lnb, tm=256, per_row_gate=False)
        kp_l.append(kT); vp_l.append(vT); fp_l.append(logfT); sp_l.append(s_p)

        (qfT, kfT, vfT, gfTs, lfT, qgT, kgT, laT, vgs, ggs) = _inproj_sample(l, xs[0], shift_s, scale_s, W)
        o_fTs = _fox_sample(l, page_table, qfT, kfT, vfT, lfT, ck, cv, cf)
        s_s, o_gs = _gla_sample(l, qgT, kgT, laT, vgs, state_gla)
        xs = _merge(l, xs, o_fTs[None], gfTs[None], o_gs.reshape(1, nb, WV_GLA), ggs[None], gate_s,
                    gain, wo, lng, lnb, tm=nb, per_row_gate=True)
        ks_l.append(kfT); vs_l.append(vfT); fs_l.append(lfT); ss_l.append(s_s)

    kp = jnp.transpose(jnp.stack(kp_l).reshape(DEPTH, B, H_FOX, HD_FOX, S), (0, 1, 4, 2, 3))
    vp = jnp.transpose(jnp.stack(vp_l).reshape(DEPTH, B, H_FOX, HD_FOX, S), (0, 1, 4, 2, 3))
    fp = jnp.transpose(jnp.stack(fp_l), (0, 1, 3, 2))
    ks = jnp.transpose(jnp.stack(ks_l).reshape(DEPTH, H_FOX, HD_FOX, nb), (0, 3, 1, 2))[:, :, None]
    vs = jnp.transpose(jnp.stack(vs_l).reshape(DEPTH, H_FOX, HD_FOX, nb), (0, 3, 1, 2))[:, :, None]
    fs = jnp.transpose(jnp.stack(fs_l), (0, 2, 1))[:, :, None]
    return (xp, xs.reshape(nb, 1, D_MODEL), kp, vp, fp, jnp.stack(sp_l), ks, vs, fs, jnp.stack(ss_l))
```

```python
import jax
import jax.numpy as jnp
import numpy as np
from jax import lax
from jax.experimental import pallas as pl
from jax.experimental.pallas import tpu as pltpu

D_MODEL = 1024
DEPTH = 4
PAGE_SIZE = 128
HD_FOX = 64
W_FOX = D_MODEL // 2
H_FOX = W_FOX // HD_FOX
H_GLA = 4
DV_GLA = (D_MODEL // 2) // H_GLA
DK_GLA = DV_GLA // 2
WK_GLA = H_GLA * DK_GLA
WV_GLA = H_GLA * DV_GLA
GATE_RANK = 16
GATE_NORM = 16.0
EPS = 1e-5
DEEPNORM_ALPHA = (2.0 * DEPTH) ** 0.25
_SPLITS = (W_FOX, W_FOX, W_FOX, H_FOX, W_FOX, WK_GLA, WK_GLA, WV_GLA, GATE_RANK, WV_GLA)
_OFF = [0]
for _s in _SPLITS:
    _OFF.append(_OFF[-1] + _s)
(_QF, _KF, _VF, _FL, _GF, _QG, _KG, _VG, _GL, _GG) = _OFF[:-1]

LANES = 128
SUB_BF16 = 16
FOX_SCALE = HD_FOX ** -0.5
GLA_SCALE = DK_GLA ** -0.5
GLA_SUB = 8
GLA_CHUNK = 128
FOX_TILE = 512
VMEM_LIMIT = 56 * 1024 * 1024

F32 = jnp.float32
BF16 = jnp.bfloat16


def _cparams(sem):
    return pltpu.CompilerParams(dimension_semantics=sem, vmem_limit_bytes=VMEM_LIMIT)


def _split3(x):
    hi = x.astype(BF16).astype(F32)
    r = x - hi
    mid = r.astype(BF16).astype(F32)
    lo = (r - mid).astype(BF16).astype(F32)
    return hi, mid, lo


def _exact_prefix_mm(tri, x, left):
    hi, mid, lo = _split3(x)
    if left:
        return _mm(tri, hi.astype(BF16)) + _mm(tri, mid.astype(BF16)) + _mm(tri, lo.astype(BF16))
    return _mm(hi.astype(BF16), tri) + _mm(mid.astype(BF16), tri) + _mm(lo.astype(BF16), tri)


def _log_sigmoid(x):
    return -(jnp.maximum(-x, 0.0) + jnp.log1p(jnp.exp(-jnp.abs(x))))


def _silu(x):
    return x * (1.0 / (1.0 + jnp.exp(-x)))


def _mm(a, b):
    return jnp.dot(a, b, preferred_element_type=F32)


def _mm_nt(a, b):
    return lax.dot_general(a, b, (((1,), (1,)), ((), ())), preferred_element_type=F32)


def _mm_tn(a, b):
    return lax.dot_general(a, b, (((0,), (0,)), ((), ())), preferred_element_type=F32)


def _ada_kernel(c_ref, w_ref, b_ref, o_ref):
    c = c_ref[...]
    o_ref[...] = _mm(_silu(c).astype(BF16), w_ref[...].astype(BF16)) + b_ref[...]


def _ada_all(c_all, w_ada, b_ada):
    n = c_all.shape[0]
    return pl.pallas_call(
        _ada_kernel,
        grid=(DEPTH, 3),
        in_specs=[pl.BlockSpec((n, D_MODEL), lambda l, j: (0, 0)),
                  pl.BlockSpec((None, D_MODEL, D_MODEL), lambda l, j: (l, 0, j)),
                  pl.BlockSpec((None, 1, D_MODEL), lambda l, j: (l, 0, j))],
        out_specs=pl.BlockSpec((None, n, D_MODEL), lambda l, j: (l, 0, j)),
        out_shape=jax.ShapeDtypeStruct((DEPTH, n, 3 * D_MODEL), F32),
        compiler_params=_cparams(("arbitrary", "arbitrary")),
        name="ada_mod",
    )(c_all, w_ada, b_ada.reshape(DEPTH, 1, 3 * D_MODEL))


def _modulated_ln(x, shift, scale):
    mu = jnp.mean(x, axis=-1, keepdims=True)
    xc = x - mu
    var = jnp.mean(xc * xc, axis=-1, keepdims=True)
    return (xc * lax.rsqrt(var + EPS)) * (1.0 + scale) + shift


_R_KF, _R_QG, _R_KG, _R_VG, _R_GG, _R_GLFL = 0, 512, 768, 1024, 1536, 2048
_R_COLS = 2176
_FL_LANE = GATE_RANK
_T_QF, _T_KF, _T_VF, _T_GF, _T_FL = 0, 512, 1024, 1536, 2048
_T_ROWS = 2048 + SUB_BF16


def _inproj_prompt_kernel(x_ref, shift_ref, scale_ref, wr_ref, wt_ref, bfc_ref, bfr_ref, wgk_ref, bgk_ref,
                          qT_ref, ka_ref, vTb_ref, kT_ref, vT_ref, logfT_ref, gfT_ref,
                          qg_ref, kg_ref, vg_ref, la_ref, gg_ref, carry_ref):
    i = pl.program_id(1)
    tm = x_ref.shape[0]
    h = _modulated_ln(x_ref[...], shift_ref[...], scale_ref[...]).astype(BF16)
    r = _mm(h, wr_ref[...])
    t = _mm_nt(wt_ref[...], h)

    kT = t[_T_KF:_T_KF + W_FOX]
    vT = t[_T_VF:_T_VF + W_FOX]
    kT_ref[...] = kT
    vT_ref[...] = vT
    vTb_ref[...] = vT.astype(BF16)
    gfT_ref[...] = t[_T_GF:_T_GF + W_FOX]
    logfT_ref[...] = _log_sigmoid(t[_T_FL:_T_FL + SUB_BF16] + bfc_ref[...])[0:H_FOX]

    sub = lax.broadcasted_iota(jnp.int32, (SUB_BF16, tm), 0)
    ones3 = jnp.where(sub < 3, 1.0, 0.0).astype(BF16)
    zeros = jnp.zeros((LANES - HD_FOX - SUB_BF16, tm), BF16)
    for hh in range(H_FOX):
        qT_ref[hh, 0:HD_FOX, :] = t[_T_QF + hh * HD_FOX:_T_QF + (hh + 1) * HD_FOX].astype(BF16)
        qT_ref[hh, HD_FOX:HD_FOX + SUB_BF16, :] = ones3
        qT_ref[hh, HD_FOX + SUB_BF16:LANES, :] = zeros

    @pl.when(i == 0)
    def _():
        carry_ref[...] = jnp.zeros_like(carry_ref)

    glfl = r[:, _R_GLFL:_R_GLFL + LANES]
    logf_r = _log_sigmoid(glfl + bfr_ref[...])
    row = lax.broadcasted_iota(jnp.int32, (tm, tm), 0)
    col = lax.broadcasted_iota(jnp.int32, (tm, tm), 1)
    lower = jnp.where(col <= row, 1.0, 0.0).astype(BF16)
    carry = carry_ref[0:1, :]
    cum = _exact_prefix_mm(lower, logf_r, left=True) + carry
    carry_ref[...] = jnp.broadcast_to(carry + jnp.sum(logf_r, axis=0, keepdims=True), carry_ref.shape)
    chi, cmid, clo = _split3(cum)

    lane = lax.broadcasted_iota(jnp.int32, (tm, LANES), 1)
    for hh in range(H_FOX):
        blk = r[:, _R_KF + (hh // 2) * LANES:_R_KF + (hh // 2 + 1) * LANES]
        if hh % 2:
            blk = pltpu.roll(blk, HD_FOX, axis=1)
        src = _FL_LANE + hh
        e0 = pltpu.roll(chi, HD_FOX - src, axis=1)
        e1 = pltpu.roll(cmid, HD_FOX + 1 - src, axis=1)
        e2 = pltpu.roll(clo, HD_FOX + 2 - src, axis=1)
        extra = jnp.where(lane == HD_FOX, -e0,
                          jnp.where(lane == HD_FOX + 1, -e1, jnp.where(lane == HD_FOX + 2, -e2, 0.0)))
        ka_ref[hh] = jnp.where(lane < HD_FOX, blk, extra).astype(BF16)

    qg_ref[...] = r[:, _R_QG:_R_QG + WK_GLA]
    kg_ref[...] = r[:, _R_KG:_R_KG + WK_GLA]
    vg_ref[...] = r[:, _R_VG:_R_VG + WV_GLA]
    gg_ref[...] = r[:, _R_GG:_R_GG + WV_GLA]
    la_ref[...] = _log_sigmoid(_mm(glfl.astype(BF16), wgk_ref[...]) + bgk_ref[...]) * (1.0 / GATE_NORM)


def _inproj_prompt(l, x, shift, scale, W, tm=256):
    B, S, _ = x.shape
    nt = S // tm
    row = lambda n: pl.BlockSpec((None, tm, n), lambda b, i: (b, i, 0))
    colT = lambda n: pl.BlockSpec((None, n, tm), lambda b, i: (b, 0, i))
    lay = lambda a, c: pl.BlockSpec((None, a, c), lambda b, i: (l, 0, 0))
    out_shape = [
        jax.ShapeDtypeStruct((B, H_FOX, LANES, S), BF16),
        jax.ShapeDtypeStruct((B, H_FOX, S, LANES), BF16),
        jax.ShapeDtypeStruct((B, W_FOX, S), BF16),
        jax.ShapeDtypeStruct((B, W_FOX, S), F32),
        jax.ShapeDtypeStruct((B, W_FOX, S), F32),
        jax.ShapeDtypeStruct((B, H_FOX, S), F32),
        jax.ShapeDtypeStruct((B, W_FOX, S), F32),
        jax.ShapeDtypeStruct((B, S, WK_GLA), F32),
        jax.ShapeDtypeStruct((B, S, WK_GLA), F32),
        jax.ShapeDtypeStruct((B, S, WV_GLA), F32),
        jax.ShapeDtypeStruct((B, S, WK_GLA), F32),
        jax.ShapeDtypeStruct((B, S, WV_GLA), F32),
    ]
    out_specs = [
        pl.BlockSpec((None, H_FOX, LANES, tm), lambda b, i: (b, 0, 0, i)),
        pl.BlockSpec((None, H_FOX, tm, LANES), lambda b, i: (b, 0, i, 0)),
        colT(W_FOX), colT(W_FOX), colT(W_FOX), colT(H_FOX), colT(W_FOX),
        row(WK_GLA), row(WK_GLA), row(WV_GLA), row(WK_GLA), row(WV_GLA),
    ]
    in_specs = [
        pl.BlockSpec((None, tm, D_MODEL), lambda b, i: (b, i, 0)),
        pl.BlockSpec((None, 1, D_MODEL), lambda b, i: (b, 0, 0)),
        pl.BlockSpec((None, 1, D_MODEL), lambda b, i: (b, 0, 0)),
        lay(D_MODEL, _R_COLS), lay(_T_ROWS, D_MODEL), lay(SUB_BF16, 1), lay(1, LANES),
        lay(LANES, WK_GLA), lay(1, WK_GLA),
    ]
    return pl.pallas_call(
        _inproj_prompt_kernel,
        grid=(B, nt),
        in_specs=in_specs,
        out_specs=out_specs,
        out_shape=out_shape,
        scratch_shapes=[pltpu.VMEM((8, LANES), F32)],
        compiler_params=_cparams(("arbitrary", "arbitrary")),
        name="inproj_prompt",
    )(x, shift, scale, W["wr_p"], W["wt_p"], W["bf_col"], W["bf_row"], W["wgk_pad"], W["bgk_row"])


def _fox_prompt_kernel(it_ref, jt_ref, ka_ref, qT_ref, vT_ref, oT_ref, m_ref, l_ref, acc_ref):
    pidx = pl.program_id(1)
    i = it_ref[pidx]
    j = jt_ref[pidx]
    tk = ka_ref.shape[1]
    tq = qT_ref.shape[2]

    @pl.when(j == 0)
    def _():
        m_ref[...] = jnp.full_like(m_ref, -jnp.inf)
        l_ref[...] = jnp.zeros_like(l_ref)
        acc_ref[...] = jnp.zeros_like(acc_ref)

    def run(masked):
        def body(hp, carry):
            for u in range(2):
                hh = 2 * hp + u
                sT = _mm(ka_ref[hh], qT_ref[hh])
                if masked:
                    kpos = lax.broadcasted_iota(jnp.int32, (tk, tq), 0)
                    qpos = lax.broadcasted_iota(jnp.int32, (tk, tq), 1)
                    sT = jnp.where(kpos <= qpos, sT, -jnp.inf)
                m_prev = m_ref[hh]
                m_new = jnp.maximum(m_prev, jnp.max(sT, axis=0, keepdims=True))
                alpha = jnp.exp(m_prev - m_new)
                pT = jnp.exp(sT - m_new)
                l_ref[hh] = alpha * l_ref[hh] + jnp.sum(pT, axis=0, keepdims=True)
                m_ref[hh] = m_new
                rows = pl.ds(pl.multiple_of(hh * HD_FOX, HD_FOX), HD_FOX)
                acc_ref[rows, :] = acc_ref[rows, :] * alpha + _mm(vT_ref[rows, :], pT.astype(BF16))
            return carry
        lax.fori_loop(0, H_FOX // 2, body, 0)

    @pl.when(j < i)
    def _():
        run(False)

    @pl.when(j == i)
    def _():
        run(True)
        for hh in range(H_FOX):
            rows = slice(hh * HD_FOX, (hh + 1) * HD_FOX)
            oT_ref[rows, :] = acc_ref[rows, :] * (1.0 / l_ref[hh])


def _fox_prompt(qT, ka, vTb):
    B, _, S, _ = ka.shape
    t = min(FOX_TILE, S)
    n = S // t
    pairs = [(i, j) for i in range(n) for j in range(i + 1)]
    it = jnp.asarray(np.array([p[0] for p in pairs], np.int32))
    jt = jnp.asarray(np.array([p[1] for p in pairs], np.int32))
    grid_spec = pltpu.PrefetchScalarGridSpec(
        num_scalar_prefetch=2, grid=(B, len(pairs)),
        in_specs=[pl.BlockSpec((None, H_FOX, t, LANES), lambda b, p, it, jt: (b, 0, jt[p], 0)),
                  pl.BlockSpec((None, H_FOX, LANES, t), lambda b, p, it, jt: (b, 0, 0, it[p])),
                  pl.BlockSpec((None, W_FOX, t), lambda b, p, it, jt: (b, 0, jt[p]))],
        out_specs=pl.BlockSpec((None, W_FOX, t), lambda b, p, it, jt: (b, 0, it[p])),
        scratch_shapes=[pltpu.VMEM((H_FOX, 1, t), F32), pltpu.VMEM((H_FOX, 1, t), F32),
                        pltpu.VMEM((W_FOX, t), F32)])
    return pl.pallas_call(
        _fox_prompt_kernel,
        grid_spec=grid_spec,
        out_shape=jax.ShapeDtypeStruct((B, W_FOX, S), F32),
        compiler_params=_cparams(("arbitrary", "arbitrary")),
        name="fox_prompt",
    )(it, jt, ka, qT, vTb)


def _decode_one(sidx, qT_ref, kT_ref, vT_ref, lfT_ref, k_pages, v_pages, f_pages, oT_ref):
    n_pages = len(k_pages)
    nb = qT_ref.shape[1]
    P = PAGE_SIZE
    R = n_pages * H_FOX

    onehot = lax.broadcasted_iota(jnp.int32, (1, nb), 1) == sidx
    pick = lambda ref: jnp.sum(jnp.where(onehot, ref[...], 0.0), axis=1, keepdims=True)
    q_col = pick(qT_ref)
    k_col = pick(kT_ref)
    v_col = pick(vT_ref)
    lf_new = pick(lfT_ref)
    q_b = [jnp.broadcast_to(q_col[hh * HD_FOX:(hh + 1) * HD_FOX], (HD_FOX, P)) for hh in range(H_FOX)]

    sub8 = lax.broadcasted_iota(jnp.int32, (H_FOX, P), 0)
    s_pages = []
    for g in range(n_pages):
        s = jnp.zeros((H_FOX, P), F32)
        for hh in range(H_FOX):
            s = jnp.where(sub8 == hh, jnp.sum(k_pages[g][hh] * q_b[hh], axis=0, keepdims=True), s)
        s_pages.append(s)
    s_all = jnp.concatenate(s_pages, axis=0)
    lf_all = jnp.concatenate([f_pages[g][...] for g in range(n_pages)], axis=0)

    upper = jnp.where(lax.broadcasted_iota(jnp.int32, (P, P), 0) <= lax.broadcasted_iota(jnp.int32, (P, P), 1),
                      1.0, 0.0).astype(BF16)
    within = _exact_prefix_mm(upper, lf_all, left=False)
    tot = jnp.sum(lf_all, axis=1, keepdims=True)
    rr = lax.broadcasted_iota(jnp.int32, (R, R), 0)
    cc = lax.broadcasted_iota(jnp.int32, (R, R), 1)
    earlier = jnp.where(((rr % H_FOX) == (cc % H_FOX)) & (cc < rr), 1.0, 0.0).astype(BF16)
    before = _exact_prefix_mm(earlier, jnp.broadcast_to(tot, (R, P)), left=True)
    x_all = s_all - (within + before)
    total = before[R - H_FOX:R, 0:1] + tot[R - H_FOX:R]

    qk = q_col * k_col
    s_self = jnp.concatenate(
        [jnp.sum(qk[hh * HD_FOX:(hh + 1) * HD_FOX], axis=0, keepdims=True) for hh in range(H_FOX)], axis=0)
    x_self = s_self - (total + lf_new)

    rowmax = jnp.max(x_all, axis=1, keepdims=True)
    m = x_self
    for g in range(n_pages):
        m = jnp.maximum(m, rowmax[g * H_FOX:(g + 1) * H_FOX])
    p_all = jnp.exp(x_all - jnp.concatenate([m] * n_pages, axis=0))
    p_self = jnp.exp(x_self - m)
    rowsum = jnp.sum(p_all, axis=1, keepdims=True)
    l = p_self
    for g in range(n_pages):
        l = l + rowsum[g * H_FOX:(g + 1) * H_FOX]
    inv_l = 1.0 / l

    outs = []
    for hh in range(H_FOX):
        acc = v_pages[0][hh] * p_all[hh:hh + 1]
        for g in range(1, n_pages):
            acc = acc + v_pages[g][hh] * p_all[g * H_FOX + hh:g * H_FOX + hh + 1]
        o = jnp.sum(acc, axis=1, keepdims=True) + p_self[hh:hh + 1] * v_col[hh * HD_FOX:(hh + 1) * HD_FOX]
        outs.append(o * inv_l[hh:hh + 1])
    o_col = jnp.concatenate(outs, axis=0)
    oT_ref[...] = jnp.where(onehot, o_col, oT_ref[...])


def _gla_decode_kernel(pt_ref, q_ref, k_ref, v_ref, la_ref, sel_ref, til_ref,
                       qTs_ref, kTs_ref, vTs_ref, lfTs_ref, *refs):
    n_pages = (len(refs) - 6) // 3
    k_pages = refs[0:n_pages]
    v_pages = refs[n_pages:2 * n_pages]
    f_pages = refs[2 * n_pages:3 * n_pages]
    o_ref, sfin_ref, oTs_ref, b_scr, k_scr, state_ref = refs[3 * n_pages:]
    c = pl.program_id(1)
    L = q_ref.shape[0]
    nsub = L // GLA_SUB
    nb = qTs_ref.shape[1]
    sidx = pl.program_id(0) * pl.num_programs(1) + c

    @pl.when(sidx == 0)
    def _():
        oTs_ref[...] = jnp.zeros_like(oTs_ref)

    @pl.when(c == 0)
    def _():
        state_ref[...] = jnp.zeros_like(state_ref)

    _decode_one(jnp.minimum(sidx, nb - 1), qTs_ref, kTs_ref, vTs_ref, lfTs_ref, k_pages, v_pages, f_pages, oTs_ref)

    q = q_ref[...]
    k = k_ref[...]

    r_i = lax.broadcasted_iota(jnp.int32, (L, L), 0)
    c_i = lax.broadcasted_iota(jnp.int32, (L, L), 1)
    lower = jnp.where(c_i <= r_i, 1.0, 0.0).astype(BF16)
    b = _exact_prefix_mm(lower, la_ref[...], left=True)
    b_scr[...] = b
    k_scr[...] = k

    head_of_lane = lax.broadcasted_iota(jnp.int32, (L, WK_GLA), 1) // DK_GLA

    def rows_bcast(ref, first, stride, n, reps):
        return jnp.concatenate(
            [jnp.broadcast_to(ref[first + stride * s:first + stride * s + 1, :], (reps, WK_GLA)) for s in range(n)],
            axis=0)

    a8 = jnp.zeros((L, H_GLA * GLA_SUB), F32)
    for j in range(GLA_SUB):
        bj = rows_bcast(b_scr, j, GLA_SUB, nsub, GLA_SUB)
        kj = rows_bcast(k_scr, j, GLA_SUB, nsub, GLA_SUB)
        p = q * kj * jnp.exp(jnp.minimum(b - bj, 0.0))
        a8 = a8 + _mm(p.astype(BF16), sel_ref[j])
    a_exp = _mm(a8.astype(BF16), til_ref[...])
    diag_mask = ((r_i // GLA_SUB) == (c_i // GLA_SUB)) & ((c_i % GLA_SUB) <= (r_i % GLA_SUB))
    a_heads = [jnp.where(diag_mask, a_exp[:, hh * L:(hh + 1) * L], 0.0) for hh in range(H_GLA)]

    m = GLA_SUB
    while m < L:
        bref = rows_bcast(b_scr, m - 1, 2 * m, L // (2 * m), 2 * m)
        e = jnp.exp(-jnp.abs(b - bref))
        qe = q * e
        ke = (k * e).astype(BF16)
        qs = jnp.concatenate([jnp.where(head_of_lane == hh, qe, 0.0).astype(BF16) for hh in range(H_GLA)], axis=0)
        prod = _mm_nt(qs, ke)
        mask = ((r_i // (2 * m)) == (c_i // (2 * m))) & ((r_i % (2 * m)) >= m) & ((c_i % (2 * m)) < m)
        a_heads = [jnp.where(mask, prod[hh * L:(hh + 1) * L], a_heads[hh]) for hh in range(H_GLA)]
        m *= 2

    state = state_ref[...]
    o_inter = _mm((q * jnp.exp(b)).astype(BF16), state.astype(BF16))
    vb = v_ref[...].astype(BF16)
    for hh in range(H_GLA):
        cols = slice(hh * DV_GLA, (hh + 1) * DV_GLA)
        o_ref[:, cols] = _mm(a_heads[hh].astype(BF16), vb[:, cols]) + o_inter[:, cols]

    b_last = b_scr[L - 1:L, :]
    upd = _mm_tn((k * jnp.exp(b_last - b)).astype(BF16), vb)
    rr = lax.broadcasted_iota(jnp.int32, (WK_GLA, WV_GLA), 0) // DK_GLA
    cc = lax.broadcasted_iota(jnp.int32, (WK_GLA, WV_GLA), 1) // DV_GLA
    eye = (lax.broadcasted_iota(jnp.int32, (WK_GLA, WK_GLA), 0)
           == lax.broadcasted_iota(jnp.int32, (WK_GLA, WK_GLA), 1))
    dec_col = jnp.sum(jnp.where(eye, jnp.broadcast_to(jnp.exp(b_last), (WK_GLA, WK_GLA)), 0.0),
                      axis=1, keepdims=True)
    new_state = jnp.where(rr == cc, dec_col * state + upd, 0.0)
    state_ref[...] = new_state

    @pl.when(c == pl.num_programs(1) - 1)
    def _():
        for hh in range(H_GLA):
            sfin_ref[hh] = new_state[hh * DK_GLA:(hh + 1) * DK_GLA, hh * DV_GLA:(hh + 1) * DV_GLA]


def _gla_consts():
    L = GLA_CHUNK
    n = H_GLA * GLA_SUB
    hk = np.arange(WK_GLA) // DK_GLA
    cols = np.arange(n)
    j = np.arange(GLA_SUB)
    sel = ((hk[None, :, None] == (cols // GLA_SUB)[None, None, :])
           & (j[:, None, None] == (cols % GLA_SUB)[None, None, :]))
    out = np.arange(H_GLA * L)
    til = (((cols // GLA_SUB)[:, None] == (out // L)[None, :])
           & ((cols % GLA_SUB)[:, None] == ((out % L) % GLA_SUB)[None, :]))
    return jnp.asarray(sel, BF16), jnp.asarray(til, BF16)


def _gla_decode(l, q, k, v, la, page_table, qTs, kTs, vTs, lfTs, ck, cv, cf):
    B, S, _ = q.shape
    L = GLA_CHUNK
    nc = S // L
    nb, n_pages = page_table.shape
    assert nb <= B * nc, "one sample sequence per grid step"
    sel, til = _gla_consts()
    seq = lambda b, c: jnp.minimum(b * nc + c, nb - 1)
    row = lambda n: pl.BlockSpec((None, L, n), lambda b, c, pt: (b, c, 0))
    small = lambda a: pl.BlockSpec((a, nb), lambda b, c, pt: (0, 0))
    kv_spec = lambda g: pl.BlockSpec((None, None, H_FOX, HD_FOX, PAGE_SIZE),
                                     lambda b, c, pt: (l, pt[seq(b, c), g], 0, 0, 0))
    f_spec = lambda g: pl.BlockSpec((None, None, H_FOX, PAGE_SIZE), lambda b, c, pt: (l, pt[seq(b, c), g], 0, 0))
    in_specs = ([row(WK_GLA), row(WK_GLA), row(WV_GLA), row(WK_GLA),
                 pl.BlockSpec(sel.shape, lambda b, c, pt: (0, 0, 0)),
                 pl.BlockSpec(til.shape, lambda b, c, pt: (0, 0)),
                 small(W_FOX), small(W_FOX), small(W_FOX), small(H_FOX)]
                + [kv_spec(g) for g in range(n_pages)]
                + [kv_spec(g) for g in range(n_pages)]
                + [f_spec(g) for g in range(n_pages)])
    grid_spec = pltpu.PrefetchScalarGridSpec(
        num_scalar_prefetch=1, grid=(B, nc), in_specs=in_specs,
        out_specs=[row(WV_GLA),
                   pl.BlockSpec((None, H_GLA, DK_GLA, DV_GLA), lambda b, c, pt: (b, 0, 0, 0)),
                   pl.BlockSpec((W_FOX, nb), lambda b, c, pt: (0, 0))],
        scratch_shapes=[pltpu.VMEM((L, WK_GLA), F32), pltpu.VMEM((L, WK_GLA), F32),
                        pltpu.VMEM((WK_GLA, WV_GLA), F32)])
    return pl.pallas_call(
        _gla_decode_kernel,
        grid_spec=grid_spec,
        out_shape=[jax.ShapeDtypeStruct((B, S, WV_GLA), F32),
                   jax.ShapeDtypeStruct((B, H_GLA, DK_GLA, DV_GLA), F32),
                   jax.ShapeDtypeStruct((W_FOX, nb), F32)],
        compiler_params=_cparams(("arbitrary", "arbitrary")),
        name="gla_decode",
    )(page_table, q, k, v, la, sel, til, qTs, kTs, vTs, lfTs,
      *([ck] * n_pages), *([cv] * n_pages), *([cf] * n_pages))


def _merge_kernel(x_ref, ofT_ref, gfT_ref, og_ref, gg_ref, gate_ref, gain_ref, wo_ref, lng_ref, lnb_ref, y_ref):
    x = x_ref[...]
    mix_fT = (ofT_ref[...] * _silu(gfT_ref[...])).astype(BF16)
    og = og_ref[...]
    gg = gg_ref[...]
    gain = gain_ref[...]
    parts = []
    for hh in range(H_GLA):
        cols = slice(hh * DV_GLA, (hh + 1) * DV_GLA)
        o = og[:, cols]
        o = o * lax.rsqrt(jnp.mean(o * o, axis=-1, keepdims=True) + EPS) * gain
        parts.append((o * _silu(gg[:, cols])).astype(BF16))
    mix_g = jnp.concatenate(parts, axis=-1)
    y = _mm_tn(mix_fT, wo_ref[0:W_FOX, :]) + _mm(mix_g, wo_ref[W_FOX:D_MODEL, :])
    z = DEEPNORM_ALPHA * x + gate_ref[...] * y
    mu = jnp.mean(z, axis=-1, keepdims=True)
    zc = z - mu
    var = jnp.mean(zc * zc, axis=-1, keepdims=True)
    y_ref[...] = zc * lax.rsqrt(var + EPS) * lng_ref[...] + lnb_ref[...]


def _merge(l, x, ofT, gfT, og, gg, gate, gain, wo, lng, lnb, tm, per_row_gate):
    B, S, _ = x.shape
    row = lambda n: pl.BlockSpec((None, tm, n), lambda b, i: (b, i, 0))
    colT = lambda n: pl.BlockSpec((None, n, tm), lambda b, i: (b, 0, i))
    gate_spec = row(D_MODEL) if per_row_gate else pl.BlockSpec((None, 1, D_MODEL), lambda b, i: (b, 0, 0))
    vec = lambda n: pl.BlockSpec((None, 1, n), lambda b, i: (l, 0, 0))
    return pl.pallas_call(
        _merge_kernel,
        grid=(B, S // tm),
        in_specs=[row(D_MODEL), colT(W_FOX), colT(W_FOX), row(WV_GLA), row(WV_GLA), gate_spec,
                  vec(DV_GLA), pl.BlockSpec((None, D_MODEL, D_MODEL), lambda b, i: (l, 0, 0)),
                  vec(D_MODEL), vec(D_MODEL)],
        out_specs=row(D_MODEL),
        out_shape=jax.ShapeDtypeStruct((B, S, D_MODEL), F32),
        compiler_params=_cparams(("arbitrary", "arbitrary")),
        name="merge",
    )(x, ofT, gfT, og, gg, gate, gain, wo, lng, lnb)


_ST_QF, _ST_KF, _ST_VF, _ST_GF, _ST_QG, _ST_KG, _ST_GL, _ST_FL = 0, 512, 1024, 1536, 2048, 2304, 2560, 2576
_ST_ROWS = 2592
_SR_VG, _SR_GG = 0, 512
_SR_COLS = 1024


def _inproj_sample_kernel(x_ref, shift_ref, scale_ref, wr_ref, wt_ref, bf_ref, wgkT_ref, bgk_ref,
                          qfT_ref, kfT_ref, vfT_ref, gfT_ref, logfT_ref, qgT_ref, kgT_ref, laT_ref,
                          vg_ref, gg_ref):
    h = _modulated_ln(x_ref[...], shift_ref[...], scale_ref[...]).astype(BF16)
    r = _mm(h, wr_ref[...])
    t = _mm_nt(wt_ref[...], h)
    qfT_ref[...] = t[_ST_QF:_ST_QF + W_FOX]
    kfT_ref[...] = t[_ST_KF:_ST_KF + W_FOX]
    vfT_ref[...] = t[_ST_VF:_ST_VF + W_FOX]
    gfT_ref[...] = t[_ST_GF:_ST_GF + W_FOX]
    qgT_ref[...] = t[_ST_QG:_ST_QG + WK_GLA]
    kgT_ref[...] = t[_ST_KG:_ST_KG + WK_GLA]
    logfT_ref[...] = _log_sigmoid(t[_ST_FL:_ST_FL + SUB_BF16] + bf_ref[...])[0:H_FOX]
    glT = t[_ST_GL:_ST_GL + GATE_RANK].astype(BF16)
    laT_ref[...] = _log_sigmoid(_mm(wgkT_ref[...], glT) + bgk_ref[...]) * (1.0 / GATE_NORM)
    vg_ref[...] = r[:, _SR_VG:_SR_VG + WV_GLA]
    gg_ref[...] = r[:, _SR_GG:_SR_GG + WV_GLA]


def _inproj_sample(l, x, shift, scale, W):
    n = x.shape[0]
    full = lambda a, b: pl.BlockSpec((a, b), lambda g: (0, 0))
    lay = lambda a, b: pl.BlockSpec((None, a, b), lambda g: (l, 0, 0))
    f = lambda a, b: jax.ShapeDtypeStruct((a, b), F32)
    return pl.pallas_call(
        _inproj_sample_kernel,
        grid=(1,),
        in_specs=[full(n, D_MODEL), full(n, D_MODEL), full(n, D_MODEL),
                  lay(D_MODEL, _SR_COLS), lay(_ST_ROWS, D_MODEL), lay(SUB_BF16, 1),
                  lay(WK_GLA, GATE_RANK), lay(WK_GLA, 1)],
        out_specs=[full(W_FOX, n), full(W_FOX, n), full(W_FOX, n), full(W_FOX, n), full(H_FOX, n),
                   full(WK_GLA, n), full(WK_GLA, n), full(WK_GLA, n),
                   full(n, WV_GLA), full(n, WV_GLA)],
        out_shape=[f(W_FOX, n), f(W_FOX, n), f(W_FOX, n), f(W_FOX, n), f(H_FOX, n),
                   f(WK_GLA, n), f(WK_GLA, n), f(WK_GLA, n),
                   f(n, WV_GLA), f(n, WV_GLA)],
        compiler_params=_cparams(("arbitrary",)),
        name="inproj_sample",
    )(x, shift, scale, W["wr_s"], W["wt_s"], W["bf_col"], W["wgkT"], W["bgk_col"])


GLA_SAMPLE_BATCH = 8


def _gla_sample_kernel(qT_ref, kT_ref, laT_ref, v_ref, s_ref, snew_ref, o_ref):
    nb = qT_ref.shape[1]
    bs = s_ref.shape[0]
    qT = qT_ref[...]
    kT = kT_ref[...]
    aT = jnp.exp(laT_ref[...])
    lane = lax.broadcasted_iota(jnp.int32, (1, nb), 1)
    for u in range(bs):
        onehot = lane == pl.program_id(0) * bs + u
        pick = lambda a: jnp.sum(jnp.where(onehot, a, 0.0), axis=1, keepdims=True)
        q_col, k_col, a_col = pick(qT), pick(kT), pick(aT)
        v = v_ref[u]
        for hh in range(H_GLA):
            rows = slice(hh * DK_GLA, (hh + 1) * DK_GLA)
            cols = slice(hh * DV_GLA, (hh + 1) * DV_GLA)
            st = a_col[rows] * s_ref[u, hh] + k_col[rows] * v[:, cols]
            snew_ref[u, hh] = st
            o_ref[u, :, cols] = jnp.sum(q_col[rows] * st, axis=0, keepdims=True)


def _gla_sample(l, qT, kT, laT, v, state):
    nb = qT.shape[1]
    bs = GLA_SAMPLE_BATCH
    small = pl.BlockSpec((WK_GLA, nb), lambda g: (0, 0))
    return pl.pallas_call(
        _gla_sample_kernel,
        grid=(nb // bs,),
        in_specs=[small, small, small,
                  pl.BlockSpec((bs, 1, WV_GLA), lambda g: (g, 0, 0)),
                  pl.BlockSpec((None, bs, H_GLA, DK_GLA, DV_GLA), lambda g: (l, g, 0, 0, 0))],
        out_specs=[pl.BlockSpec((bs, H_GLA, DK_GLA, DV_GLA), lambda g: (g, 0, 0, 0)),
                   pl.BlockSpec((bs, 1, WV_GLA), lambda g: (g, 0, 0))],
        out_shape=[jax.ShapeDtypeStruct((nb, H_GLA, DK_GLA, DV_GLA), F32),
                   jax.ShapeDtypeStruct((nb, 1, WV_GLA), F32)],
        compiler_params=_cparams(("arbitrary",)),
        name="gla_sample",
    )(qT, kT, laT, v.reshape(nb, 1, WV_GLA), state)


def _prep_weights(w_in, b_f, w_gk, b_gk):
    sl = lambda off, n: w_in[:, :, off:off + n]
    qf = sl(_QF, W_FOX) * FOX_SCALE
    qg = sl(_QG, WK_GLA) * GLA_SCALE
    kf, vf, fl, gf = sl(_KF, W_FOX), sl(_VF, W_FOX), sl(_FL, H_FOX), sl(_GF, W_FOX)
    kg, vg, gl, gg = sl(_KG, WK_GLA), sl(_VG, WV_GLA), sl(_GL, GATE_RANK), sl(_GG, WV_GLA)
    zc = lambda n: jnp.zeros((DEPTH, D_MODEL, n), F32)
    tr = lambda a: jnp.transpose(a, (0, 2, 1))
    fl_pad = jnp.concatenate([fl, zc(SUB_BF16 - H_FOX)], axis=2)
    glfl = jnp.concatenate([gl, fl, zc(LANES - GATE_RANK - H_FOX)], axis=2)
    wr_p = jnp.concatenate([kf, qg, kg, vg, gg, glfl], axis=2).astype(BF16)
    wt_p = tr(jnp.concatenate([qf, kf, vf, gf, fl_pad], axis=2)).astype(BF16)
    wr_s = jnp.concatenate([vg, gg], axis=2).astype(BF16)
    wt_s = tr(jnp.concatenate([qf, kf, vf, gf, qg, kg, gl, fl_pad], axis=2)).astype(BF16)
    zf = lambda n: jnp.zeros((DEPTH, n), F32)
    bf_col = jnp.concatenate([b_f, zf(SUB_BF16 - H_FOX)], axis=1)[:, :, None]
    bf_row = jnp.concatenate([zf(_FL_LANE), b_f, zf(LANES - _FL_LANE - H_FOX)], axis=1)[:, None, :]
    wgk_pad = jnp.concatenate([w_gk, jnp.zeros((DEPTH, LANES - GATE_RANK, WK_GLA), F32)], axis=1).astype(BF16)
    wgkT = tr(w_gk).astype(BF16)
    return dict(wr_p=wr_p, wt_p=wt_p, wr_s=wr_s, wt_s=wt_s, bf_col=bf_col, bf_row=bf_row,
                wgk_pad=wgk_pad, wgkT=wgkT, bgk_row=b_gk[:, None, :], bgk_col=b_gk[:, :, None])


def kernel(x_prompt, x_sample, cache_k, cache_v, cache_logf, state_gla, page_table, c_prompt, c_sample,
           w_ada, b_ada, w_in, b_f, w_gk, b_gk, gla_gain, w_out, ln_g, ln_b):
    B, S, _ = x_prompt.shape
    nb = x_sample.shape[0]
    W = _prep_weights(w_in, b_f, w_gk, b_gk)
    wo = w_out.astype(BF16)
    gain = gla_gain[:, None, :]
    lng = ln_g[:, None, :]
    lnb = ln_b[:, None, :]

    mod = _ada_all(jnp.concatenate([c_prompt, c_sample], axis=0), w_ada, b_ada)
    ck = jnp.transpose(cache_k, (0, 1, 3, 4, 2))
    cv = jnp.transpose(cache_v, (0, 1, 3, 4, 2))
    cf = jnp.transpose(cache_logf, (0, 1, 3, 2))

    xp = x_prompt
    xs = x_sample.reshape(1, nb, D_MODEL)
    kp_l, vp_l, fp_l, sp_l, ks_l, vs_l, fs_l, ss_l = ([] for _ in range(8))
    for l in range(DEPTH):
        shift_p = mod[l, :B, None, 0:D_MODEL]
        scale_p = mod[l, :B, None, D_MODEL:2 * D_MODEL]
        gate_p = mod[l, :B, None, 2 * D_MODEL:]
        shift_s = mod[l, B:, 0:D_MODEL]
        scale_s = mod[l, B:, D_MODEL:2 * D_MODEL]
        gate_s = mod[l, B:, 2 * D_MODEL:][None]

        (qT, ka, vTb, kT, vT, logfT, gfT, qg, kg, vg, la, gg) = _inproj_prompt(l, xp, shift_p, scale_p, W)
        (qfT, kfT, vfT, gfTs, lfT, qgT, kgT, laT, vgs, ggs) = _inproj_sample(l, xs[0], shift_s, scale_s, W)
        o_fT = _fox_prompt(qT, ka, vTb)
        o_g, s_p, o_fTs = _gla_decode(l, qg, kg, vg, la, page_table, qfT, kfT, vfT, lfT, ck, cv, cf)
        xp = _merge(l, xp, o_fT, gfT, o_g, gg, gate_p, gain, wo, lng, lnb, tm=256, per_row_gate=False)
        kp_l.append(kT); vp_l.append(vT); fp_l.append(logfT); sp_l.append(s_p)

        s_s, o_gs = _gla_sample(l, qgT, kgT, laT, vgs, state_gla)
        xs = _merge(l, xs, o_fTs[None], gfTs[None], o_gs.reshape(1, nb, WV_GLA), ggs[None], gate_s,
                    gain, wo, lng, lnb, tm=nb, per_row_gate=True)
        ks_l.append(kfT); vs_l.append(vfT); fs_l.append(lfT); ss_l.append(s_s)

    kp = jnp.transpose(jnp.stack(kp_l).reshape(DEPTH, B, H_FOX, HD_FOX, S), (0, 1, 4, 2, 3))
    vp = jnp.transpose(jnp.stack(vp_l).reshape(DEPTH, B, H_FOX, HD_FOX, S), (0, 1, 4, 2, 3))
    fp = jnp.transpose(jnp.stack(fp_l), (0, 1, 3, 2))
    ks = jnp.transpose(jnp.stack(ks_l).reshape(DEPTH, H_FOX, HD_FOX, nb), (0, 3, 1, 2))[:, :, None]
    vs = jnp.transpose(jnp.stack(vs_l).reshape(DEPTH, H_FOX, HD_FOX, nb), (0, 3, 1, 2))[:, :, None]
    fs = jnp.transpose(jnp.stack(fs_l), (0, 2, 1))[:, :, None]
    return (xp, xs.reshape(nb, 1, D_MODEL), kp, vp, fp, jnp.stack(sp_l), ks, vs, fs, jnp.stack(ss_l))
```

```python
import functools

import jax
import jax.numpy as jnp
import numpy as np
from jax import lax
from jax.experimental import pallas as pl
from jax.experimental.pallas import tpu as pltpu

D_MODEL = 1024
DEPTH = 4
PAGE_SIZE = 128
HD_FOX = 64
W_FOX = D_MODEL // 2
H_FOX = W_FOX // HD_FOX
H_GLA = 4
DV_GLA = (D_MODEL // 2) // H_GLA
DK_GLA = DV_GLA // 2
WK_GLA = H_GLA * DK_GLA
WV_GLA = H_GLA * DV_GLA
GATE_RANK = 16
GATE_NORM = 16.0
EPS = 1e-5
DEEPNORM_ALPHA = (2.0 * DEPTH) ** 0.25
_SPLITS = (W_FOX, W_FOX, W_FOX, H_FOX, W_FOX, WK_GLA, WK_GLA, WV_GLA, GATE_RANK, WV_GLA)
_OFF = [0]
for _s in _SPLITS:
    _OFF.append(_OFF[-1] + _s)
(_QF, _KF, _VF, _FL, _GF, _QG, _KG, _VG, _GL, _GG) = _OFF[:-1]

LANES = 128
SUB_BF16 = 16
FOX_SCALE = HD_FOX ** -0.5
GLA_SCALE = DK_GLA ** -0.5
GLA_SUB = 8
GLA_CHUNK = 128
FOX_TILE = 512
VMEM_LIMIT = 56 * 1024 * 1024

F32 = jnp.float32
BF16 = jnp.bfloat16


def _cparams(sem):
    return pltpu.CompilerParams(dimension_semantics=sem, vmem_limit_bytes=VMEM_LIMIT)


def _split3(x):
    hi = x.astype(BF16).astype(F32)
    r = x - hi
    mid = r.astype(BF16).astype(F32)
    lo = (r - mid).astype(BF16).astype(F32)
    return hi, mid, lo


def _exact_prefix_mm(tri, x, left):
    hi, mid, lo = _split3(x)
    if left:
        return _mm(tri, hi.astype(BF16)) + _mm(tri, mid.astype(BF16)) + _mm(tri, lo.astype(BF16))
    return _mm(hi.astype(BF16), tri) + _mm(mid.astype(BF16), tri) + _mm(lo.astype(BF16), tri)


def _log_sigmoid(x):
    return -(jnp.maximum(-x, 0.0) + jnp.log1p(jnp.exp(-jnp.abs(x))))


def _silu(x):
    return x * (1.0 / (1.0 + jnp.exp(-x)))


def _mm(a, b):
    return jnp.dot(a, b, preferred_element_type=F32)


def _mm_nt(a, b):
    return lax.dot_general(a, b, (((1,), (1,)), ((), ())), preferred_element_type=F32)


def _mm_tn(a, b):
    return lax.dot_general(a, b, (((0,), (0,)), ((), ())), preferred_element_type=F32)


def _ada_kernel(c_ref, w_ref, b_ref, o_ref):
    c = c_ref[...]
    o_ref[...] = _mm(_silu(c).astype(BF16), w_ref[...].astype(BF16)) + b_ref[...]


def _ada_all(c_all, w_ada, b_ada):
    n = c_all.shape[0]
    return pl.pallas_call(
        _ada_kernel,
        grid=(DEPTH, 3),
        in_specs=[pl.BlockSpec((n, D_MODEL), lambda l, j: (0, 0)),
                  pl.BlockSpec((None, D_MODEL, D_MODEL), lambda l, j: (l, 0, j)),
                  pl.BlockSpec((None, 1, D_MODEL), lambda l, j: (l, 0, j))],
        out_specs=pl.BlockSpec((None, n, D_MODEL), lambda l, j: (l, 0, j)),
        out_shape=jax.ShapeDtypeStruct((DEPTH, n, 3 * D_MODEL), F32),
        compiler_params=_cparams(("arbitrary", "arbitrary")),
        name="ada_mod",
    )(c_all, w_ada, b_ada.reshape(DEPTH, 1, 3 * D_MODEL))


def _modulated_ln(x, shift, scale):
    mu = jnp.mean(x, axis=-1, keepdims=True)
    xc = x - mu
    var = jnp.mean(xc * xc, axis=-1, keepdims=True)
    return (xc * lax.rsqrt(var + EPS)) * (1.0 + scale) + shift


_R_KF, _R_QG, _R_KG, _R_VG, _R_GG, _R_GLFL = 0, 512, 768, 1024, 1536, 2048
_R_COLS = 2176
_FL_LANE = GATE_RANK
_T_QF, _T_KF, _T_VF, _T_GF, _T_FL = 0, 512, 1024, 1536, 2048
_T_ROWS = 2048 + SUB_BF16


def _inproj_prompt_kernel(n_stacked, x_ref, shift_ref, scale_ref, wr_ref, wt_ref, bfc_ref, bfr_ref, wgk_ref,
                          bgk_ref, *refs):
    (qT_ref, ka_ref, vTb_ref, kT_ref, vT_ref, logfT_ref, gfT_ref,
     qg_ref, kg_ref, vg_ref, la_ref, gg_ref, carry_ref) = refs[n_stacked:]
    i = pl.program_id(1)
    tm = x_ref.shape[0]
    h = _modulated_ln(x_ref[...], shift_ref[...], scale_ref[...]).astype(BF16)
    r = _mm(h, wr_ref[...])
    t = _mm_nt(wt_ref[...], h)

    kT = t[_T_KF:_T_KF + W_FOX]
    vT = t[_T_VF:_T_VF + W_FOX]
    kT_ref[...] = kT
    vT_ref[...] = vT
    vTb_ref[...] = vT.astype(BF16)
    gfT_ref[...] = t[_T_GF:_T_GF + W_FOX]
    logfT_ref[...] = _log_sigmoid(t[_T_FL:_T_FL + SUB_BF16] + bfc_ref[...])[0:H_FOX]

    sub = lax.broadcasted_iota(jnp.int32, (SUB_BF16, tm), 0)
    ones3 = jnp.where(sub < 3, 1.0, 0.0).astype(BF16)
    zeros = jnp.zeros((LANES - HD_FOX - SUB_BF16, tm), BF16)
    for hh in range(H_FOX):
        qT_ref[hh, 0:HD_FOX, :] = t[_T_QF + hh * HD_FOX:_T_QF + (hh + 1) * HD_FOX].astype(BF16)
        qT_ref[hh, HD_FOX:HD_FOX + SUB_BF16, :] = ones3
        qT_ref[hh, HD_FOX + SUB_BF16:LANES, :] = zeros

    @pl.when(i == 0)
    def _():
        carry_ref[...] = jnp.zeros_like(carry_ref)

    glfl = r[:, _R_GLFL:_R_GLFL + LANES]
    logf_r = _log_sigmoid(glfl + bfr_ref[...])
    row = lax.broadcasted_iota(jnp.int32, (tm, tm), 0)
    col = lax.broadcasted_iota(jnp.int32, (tm, tm), 1)
    lower = jnp.where(col <= row, 1.0, 0.0).astype(BF16)
    carry = carry_ref[0:1, :]
    cum = _exact_prefix_mm(lower, logf_r, left=True) + carry
    carry_ref[...] = jnp.broadcast_to(carry + jnp.sum(logf_r, axis=0, keepdims=True), carry_ref.shape)
    chi, cmid, clo = _split3(cum)

    lane = lax.broadcasted_iota(jnp.int32, (tm, LANES), 1)
    for hh in range(H_FOX):
        blk = r[:, _R_KF + (hh // 2) * LANES:_R_KF + (hh // 2 + 1) * LANES]
        if hh % 2:
            blk = pltpu.roll(blk, HD_FOX, axis=1)
        src = _FL_LANE + hh
        e0 = pltpu.roll(chi, HD_FOX - src, axis=1)
        e1 = pltpu.roll(cmid, HD_FOX + 1 - src, axis=1)
        e2 = pltpu.roll(clo, HD_FOX + 2 - src, axis=1)
        extra = jnp.where(lane == HD_FOX, -e0,
                          jnp.where(lane == HD_FOX + 1, -e1, jnp.where(lane == HD_FOX + 2, -e2, 0.0)))
        ka_ref[hh] = jnp.where(lane < HD_FOX, blk, extra).astype(BF16)

    qg_ref[...] = r[:, _R_QG:_R_QG + WK_GLA]
    kg_ref[...] = r[:, _R_KG:_R_KG + WK_GLA]
    vg_ref[...] = r[:, _R_VG:_R_VG + WV_GLA]
    gg_ref[...] = r[:, _R_GG:_R_GG + WV_GLA]
    la_ref[...] = _log_sigmoid(_mm(glfl.astype(BF16), wgk_ref[...]) + bgk_ref[...]) * (1.0 / GATE_NORM)


def _inproj_prompt(l, x, shift, scale, W, stacked, tm=512):
    B, S, _ = x.shape
    nt = S // tm
    row = lambda n: pl.BlockSpec((None, tm, n), lambda b, i: (b, i, 0))
    colT = lambda n: pl.BlockSpec((None, n, tm), lambda b, i: (b, 0, i))
    layT = lambda n: pl.BlockSpec((None, None, n, tm), lambda b, i: (l, b, 0, i))
    lay = lambda a, c: pl.BlockSpec((None, a, c), lambda b, i: (l, 0, 0))
    out_shape = [
        jax.ShapeDtypeStruct((B, H_FOX, LANES, S), BF16),
        jax.ShapeDtypeStruct((B, H_FOX, S, LANES), BF16),
        jax.ShapeDtypeStruct((B, W_FOX, S), BF16),
        jax.ShapeDtypeStruct((DEPTH, B, W_FOX, S), F32),
        jax.ShapeDtypeStruct((DEPTH, B, W_FOX, S), F32),
        jax.ShapeDtypeStruct((DEPTH, B, H_FOX, S), F32),
        jax.ShapeDtypeStruct((B, W_FOX, S), F32),
        jax.ShapeDtypeStruct((B, S, WK_GLA), F32),
        jax.ShapeDtypeStruct((B, S, WK_GLA), F32),
        jax.ShapeDtypeStruct((B, S, WV_GLA), F32),
        jax.ShapeDtypeStruct((B, S, WK_GLA), F32),
        jax.ShapeDtypeStruct((B, S, WV_GLA), F32),
    ]
    out_specs = [
        pl.BlockSpec((None, H_FOX, LANES, tm), lambda b, i: (b, 0, 0, i)),
        pl.BlockSpec((None, H_FOX, tm, LANES), lambda b, i: (b, 0, i, 0)),
        colT(W_FOX), layT(W_FOX), layT(W_FOX), layT(H_FOX), colT(W_FOX),
        row(WK_GLA), row(WK_GLA), row(WV_GLA), row(WK_GLA), row(WV_GLA),
    ]
    in_specs = [
        pl.BlockSpec((None, tm, D_MODEL), lambda b, i: (b, i, 0)),
        pl.BlockSpec((None, 1, D_MODEL), lambda b, i: (b, 0, 0)),
        pl.BlockSpec((None, 1, D_MODEL), lambda b, i: (b, 0, 0)),
        lay(D_MODEL, _R_COLS), lay(_T_ROWS, D_MODEL), lay(SUB_BF16, 1), lay(1, LANES),
        lay(LANES, WK_GLA), lay(1, WK_GLA),
    ]
    args = [x, shift, scale, W["wr_p"], W["wt_p"], W["bf_col"], W["bf_row"], W["wgk_pad"], W["bgk_row"]]
    aliases = {}
    if stacked is not None:
        aliases = {len(args) + n: 3 + n for n in range(len(stacked))}
        in_specs = in_specs + [pl.BlockSpec(memory_space=pl.ANY)] * len(stacked)
        args = args + list(stacked)
    return pl.pallas_call(
        functools.partial(_inproj_prompt_kernel, len(aliases)),
        grid=(B, nt),
        in_specs=in_specs,
        out_specs=out_specs,
        out_shape=out_shape,
        input_output_aliases=aliases,
        scratch_shapes=[pltpu.VMEM((8, LANES), F32)],
        compiler_params=_cparams(("arbitrary", "arbitrary")),
        name="inproj_prompt",
    )(*args)


def _fox_prompt_kernel(it_ref, jt_ref, ka_ref, qT_ref, vT_ref, oT_ref, m_ref, l_ref, acc_ref):
    pidx = pl.program_id(1)
    i = it_ref[pidx]
    j = jt_ref[pidx]
    tk = ka_ref.shape[1]
    tq = qT_ref.shape[2]

    @pl.when(j == 0)
    def _():
        m_ref[...] = jnp.full_like(m_ref, -jnp.inf)
        l_ref[...] = jnp.zeros_like(l_ref)
        acc_ref[...] = jnp.zeros_like(acc_ref)

    def run(masked):
        for hh in range(H_FOX):
            sT = _mm(ka_ref[hh], qT_ref[hh])
            if masked:
                kpos = lax.broadcasted_iota(jnp.int32, (tk, tq), 0)
                qpos = lax.broadcasted_iota(jnp.int32, (tk, tq), 1)
                sT = jnp.where(kpos <= qpos, sT, -jnp.inf)
            m_prev = m_ref[hh]
            m_new = jnp.maximum(m_prev, jnp.max(sT, axis=0, keepdims=True))
            alpha = jnp.exp(m_prev - m_new)
            pT = jnp.exp(sT - m_new)
            l_ref[hh] = alpha * l_ref[hh] + jnp.sum(pT, axis=0, keepdims=True)
            m_ref[hh] = m_new
            rows = slice(hh * HD_FOX, (hh + 1) * HD_FOX)
            acc_ref[rows, :] = acc_ref[rows, :] * alpha + _mm(vT_ref[rows, :], pT.astype(BF16))

    @pl.when(j < i)
    def _():
        run(False)

    @pl.when(j == i)
    def _():
        run(True)
        for hh in range(H_FOX):
            rows = slice(hh * HD_FOX, (hh + 1) * HD_FOX)
            oT_ref[rows, :] = acc_ref[rows, :] * (1.0 / l_ref[hh])


def _fox_prompt(qT, ka, vTb):
    B, _, S, _ = ka.shape
    t = min(FOX_TILE, S)
    n = S // t
    pairs = [(i, j) for i in range(n) for j in range(i + 1)]
    it = jnp.asarray(np.array([p[0] for p in pairs], np.int32))
    jt = jnp.asarray(np.array([p[1] for p in pairs], np.int32))
    grid_spec = pltpu.PrefetchScalarGridSpec(
        num_scalar_prefetch=2, grid=(B, len(pairs)),
        in_specs=[pl.BlockSpec((None, H_FOX, t, LANES), lambda b, p, it, jt: (b, 0, jt[p], 0)),
                  pl.BlockSpec((None, H_FOX, LANES, t), lambda b, p, it, jt: (b, 0, 0, it[p])),
                  pl.BlockSpec((None, W_FOX, t), lambda b, p, it, jt: (b, 0, jt[p]))],
        out_specs=pl.BlockSpec((None, W_FOX, t), lambda b, p, it, jt: (b, 0, it[p])),
        scratch_shapes=[pltpu.VMEM((H_FOX, 1, t), F32), pltpu.VMEM((H_FOX, 1, t), F32),
                        pltpu.VMEM((W_FOX, t), F32)])
    return pl.pallas_call(
        _fox_prompt_kernel,
        grid_spec=grid_spec,
        out_shape=jax.ShapeDtypeStruct((B, W_FOX, S), F32),
        compiler_params=_cparams(("arbitrary", "arbitrary")),
        name="fox_prompt",
    )(it, jt, ka, qT, vTb)


def _decode_one(sidx, qT_ref, kT_ref, vT_ref, lfT_ref, k_pages, v_pages, f_pages, oT_ref):
    n_pages = len(k_pages)
    nb = qT_ref.shape[1]
    P = PAGE_SIZE
    R = n_pages * H_FOX

    onehot = lax.broadcasted_iota(jnp.int32, (1, nb), 1) == sidx
    pick = lambda ref: jnp.sum(jnp.where(onehot, ref[...], 0.0), axis=1, keepdims=True)
    q_col = pick(qT_ref)
    k_col = pick(kT_ref)
    v_col = pick(vT_ref)
    lf_new = pick(lfT_ref)
    q_b = [jnp.broadcast_to(q_col[hh * HD_FOX:(hh + 1) * HD_FOX], (HD_FOX, P)) for hh in range(H_FOX)]

    sub8 = lax.broadcasted_iota(jnp.int32, (H_FOX, P), 0)
    s_pages = []
    for g in range(n_pages):
        s = jnp.zeros((H_FOX, P), F32)
        for hh in range(H_FOX):
            s = jnp.where(sub8 == hh, jnp.sum(k_pages[g][hh] * q_b[hh], axis=0, keepdims=True), s)
        s_pages.append(s)
    s_all = jnp.concatenate(s_pages, axis=0)
    lf_all = jnp.concatenate([f_pages[g][...] for g in range(n_pages)], axis=0)

    upper = jnp.where(lax.broadcasted_iota(jnp.int32, (P, P), 0) <= lax.broadcasted_iota(jnp.int32, (P, P), 1),
                      1.0, 0.0).astype(BF16)
    within = _exact_prefix_mm(upper, lf_all, left=False)
    tot = jnp.sum(lf_all, axis=1, keepdims=True)
    rr = lax.broadcasted_iota(jnp.int32, (R, R), 0)
    cc = lax.broadcasted_iota(jnp.int32, (R, R), 1)
    earlier = jnp.where(((rr % H_FOX) == (cc % H_FOX)) & (cc < rr), 1.0, 0.0).astype(BF16)
    before = _exact_prefix_mm(earlier, jnp.broadcast_to(tot, (R, P)), left=True)
    x_all = s_all - (within + before)
    total = before[R - H_FOX:R, 0:1] + tot[R - H_FOX:R]

    qk = q_col * k_col
    s_self = jnp.concatenate(
        [jnp.sum(qk[hh * HD_FOX:(hh + 1) * HD_FOX], axis=0, keepdims=True) for hh in range(H_FOX)], axis=0)
    x_self = s_self - (total + lf_new)

    rowmax = jnp.max(x_all, axis=1, keepdims=True)
    m = x_self
    for g in range(n_pages):
        m = jnp.maximum(m, rowmax[g * H_FOX:(g + 1) * H_FOX])
    p_all = jnp.exp(x_all - jnp.concatenate([m] * n_pages, axis=0))
    p_self = jnp.exp(x_self - m)
    rowsum = jnp.sum(p_all, axis=1, keepdims=True)
    l = p_self
    for g in range(n_pages):
        l = l + rowsum[g * H_FOX:(g + 1) * H_FOX]
    inv_l = 1.0 / l

    outs = []
    for hh in range(H_FOX):
        acc = v_pages[0][hh] * p_all[hh:hh + 1]
        for g in range(1, n_pages):
            acc = acc + v_pages[g][hh] * p_all[g * H_FOX + hh:g * H_FOX + hh + 1]
        o = jnp.sum(acc, axis=1, keepdims=True) + p_self[hh:hh + 1] * v_col[hh * HD_FOX:(hh + 1) * HD_FOX]
        outs.append(o * inv_l[hh:hh + 1])
    o_col = jnp.concatenate(outs, axis=0)
    oT_ref[...] = jnp.where(onehot, o_col, oT_ref[...])


def _gla_decode_kernel(pt_ref, q_ref, k_ref, v_ref, la_ref, sel_ref, til_ref,
                       qTs_ref, kTs_ref, vTs_ref, lfTs_ref, *refs):
    n_pages = (len(refs) - 6) // 3
    k_pages = refs[0:n_pages]
    v_pages = refs[n_pages:2 * n_pages]
    f_pages = refs[2 * n_pages:3 * n_pages]
    o_ref, sfin_ref, oTs_ref, b_scr, k_scr, state_ref = refs[3 * n_pages:]
    c = pl.program_id(1)
    L = q_ref.shape[0]
    nsub = L // GLA_SUB
    nb = qTs_ref.shape[1]
    sidx = pl.program_id(0) * pl.num_programs(1) + c

    @pl.when(sidx == 0)
    def _():
        oTs_ref[...] = jnp.zeros_like(oTs_ref)

    @pl.when(c == 0)
    def _():
        state_ref[...] = jnp.zeros_like(state_ref)

    _decode_one(jnp.minimum(sidx, nb - 1), qTs_ref, kTs_ref, vTs_ref, lfTs_ref, k_pages, v_pages, f_pages, oTs_ref)

    q = q_ref[...]
    k = k_ref[...]

    r_i = lax.broadcasted_iota(jnp.int32, (L, L), 0)
    c_i = lax.broadcasted_iota(jnp.int32, (L, L), 1)
    lower = jnp.where(c_i <= r_i, 1.0, 0.0).astype(BF16)
    b = _exact_prefix_mm(lower, la_ref[...], left=True)
    b_scr[...] = b
    k_scr[...] = k

    head_of_lane = lax.broadcasted_iota(jnp.int32, (L, WK_GLA), 1) // DK_GLA

    def rows_bcast(ref, first, stride, n, reps):
        return jnp.concatenate(
            [jnp.broadcast_to(ref[first + stride * s:first + stride * s + 1, :], (reps, WK_GLA)) for s in range(n)],
            axis=0)

    a8 = jnp.zeros((L, H_GLA * GLA_SUB), F32)
    for j in range(GLA_SUB):
        bj = rows_bcast(b_scr, j, GLA_SUB, nsub, GLA_SUB)
        kj = rows_bcast(k_scr, j, GLA_SUB, nsub, GLA_SUB)
        p = q * kj * jnp.exp(jnp.minimum(b - bj, 0.0))
        a8 = a8 + _mm(p.astype(BF16), sel_ref[j])
    a_exp = _mm(a8.astype(BF16), til_ref[...])
    diag_mask = ((r_i // GLA_SUB) == (c_i // GLA_SUB)) & ((c_i % GLA_SUB) <= (r_i % GLA_SUB))
    a_heads = [jnp.where(diag_mask, a_exp[:, hh * L:(hh + 1) * L], 0.0) for hh in range(H_GLA)]

    m = GLA_SUB
    while m < L:
        bref = rows_bcast(b_scr, m - 1, 2 * m, L // (2 * m), 2 * m)
        e = jnp.exp(-jnp.abs(b - bref))
        qe = q * e
        ke = (k * e).astype(BF16)
        qs = jnp.concatenate([jnp.where(head_of_lane == hh, qe, 0.0).astype(BF16) for hh in range(H_GLA)], axis=0)
        prod = _mm_nt(qs, ke)
        mask = ((r_i // (2 * m)) == (c_i // (2 * m))) & ((r_i % (2 * m)) >= m) & ((c_i % (2 * m)) < m)
        a_heads = [jnp.where(mask, prod[hh * L:(hh + 1) * L], a_heads[hh]) for hh in range(H_GLA)]
        m *= 2

    state = state_ref[...]
    o_inter = _mm((q * jnp.exp(b)).astype(BF16), state.astype(BF16))
    vb = v_ref[...].astype(BF16)
    for hh in range(H_GLA):
        cols = slice(hh * DV_GLA, (hh + 1) * DV_GLA)
        o_ref[:, cols] = _mm(a_heads[hh].astype(BF16), vb[:, cols]) + o_inter[:, cols]

    b_last = b_scr[L - 1:L, :]
    upd = _mm_tn((k * jnp.exp(b_last - b)).astype(BF16), vb)
    rr = lax.broadcasted_iota(jnp.int32, (WK_GLA, WV_GLA), 0) // DK_GLA
    cc = lax.broadcasted_iota(jnp.int32, (WK_GLA, WV_GLA), 1) // DV_GLA
    eye = (lax.broadcasted_iota(jnp.int32, (WK_GLA, WK_GLA), 0)
           == lax.broadcasted_iota(jnp.int32, (WK_GLA, WK_GLA), 1))
    dec_col = jnp.sum(jnp.where(eye, jnp.broadcast_to(jnp.exp(b_last), (WK_GLA, WK_GLA)), 0.0),
                      axis=1, keepdims=True)
    new_state = jnp.where(rr == cc, dec_col * state + upd, 0.0)
    state_ref[...] = new_state

    @pl.when(c == pl.num_programs(1) - 1)
    def _():
        for hh in range(H_GLA):
            sfin_ref[hh] = new_state[hh * DK_GLA:(hh + 1) * DK_GLA, hh * DV_GLA:(hh + 1) * DV_GLA]


def _gla_consts():
    L = GLA_CHUNK
    n = H_GLA * GLA_SUB
    hk = np.arange(WK_GLA) // DK_GLA
    cols = np.arange(n)
    j = np.arange(GLA_SUB)
    sel = ((hk[None, :, None] == (cols // GLA_SUB)[None, None, :])
           & (j[:, None, None] == (cols % GLA_SUB)[None, None, :]))
    out = np.arange(H_GLA * L)
    til = (((cols // GLA_SUB)[:, None] == (out // L)[None, :])
           & ((cols % GLA_SUB)[:, None] == ((out % L) % GLA_SUB)[None, :]))
    return jnp.asarray(sel, BF16), jnp.asarray(til, BF16)


def _gla_decode(l, q, k, v, la, page_table, qTs, kTs, vTs, lfTs, ck, cv, cf):
    B, S, _ = q.shape
    L = GLA_CHUNK
    nc = S // L
    nb, n_pages = page_table.shape
    assert nb <= B * nc, "one sample sequence per grid step"
    sel, til = _gla_consts()
    seq = lambda b, c: jnp.minimum(b * nc + c, nb - 1)
    row = lambda n: pl.BlockSpec((None, L, n), lambda b, c, pt: (b, c, 0))
    small = lambda a: pl.BlockSpec((a, nb), lambda b, c, pt: (0, 0))
    kv_spec = lambda g: pl.BlockSpec((None, None, H_FOX, HD_FOX, PAGE_SIZE),
                                     lambda b, c, pt: (l, pt[seq(b, c), g], 0, 0, 0))
    f_spec = lambda g: pl.BlockSpec((None, None, H_FOX, PAGE_SIZE), lambda b, c, pt: (l, pt[seq(b, c), g], 0, 0))
    in_specs = ([row(WK_GLA), row(WK_GLA), row(WV_GLA), row(WK_GLA),
                 pl.BlockSpec(sel.shape, lambda b, c, pt: (0, 0, 0)),
                 pl.BlockSpec(til.shape, lambda b, c, pt: (0, 0)),
                 small(W_FOX), small(W_FOX), small(W_FOX), small(H_FOX)]
                + [kv_spec(g) for g in range(n_pages)]
                + [kv_spec(g) for g in range(n_pages)]
                + [f_spec(g) for g in range(n_pages)])
    grid_spec = pltpu.PrefetchScalarGridSpec(
        num_scalar_prefetch=1, grid=(B, nc), in_specs=in_specs,
        out_specs=[row(WV_GLA),
                   pl.BlockSpec((None, H_GLA, DK_GLA, DV_GLA), lambda b, c, pt: (b, 0, 0, 0)),
                   pl.BlockSpec((W_FOX, nb), lambda b, c, pt: (0, 0))],
        scratch_shapes=[pltpu.VMEM((L, WK_GLA), F32), pltpu.VMEM((L, WK_GLA), F32),
                        pltpu.VMEM((WK_GLA, WV_GLA), F32)])
    return pl.pallas_call(
        _gla_decode_kernel,
        grid_spec=grid_spec,
        out_shape=[jax.ShapeDtypeStruct((B, S, WV_GLA), F32),
                   jax.ShapeDtypeStruct((B, H_GLA, DK_GLA, DV_GLA), F32),
                   jax.ShapeDtypeStruct((W_FOX, nb), F32)],
        compiler_params=_cparams(("arbitrary", "arbitrary")),
        name="gla_decode",
    )(page_table, q, k, v, la, sel, til, qTs, kTs, vTs, lfTs,
      *([ck] * n_pages), *([cv] * n_pages), *([cf] * n_pages))


def _merge_kernel(x_ref, ofT_ref, gfT_ref, og_ref, gg_ref, gate_ref, gain_ref, wo_ref, lng_ref, lnb_ref, y_ref):
    x = x_ref[...]
    mix_fT = (ofT_ref[...] * _silu(gfT_ref[...])).astype(BF16)
    og = og_ref[...]
    gg = gg_ref[...]
    gain = gain_ref[...]
    parts = []
    for hh in range(H_GLA):
        cols = slice(hh * DV_GLA, (hh + 1) * DV_GLA)
        o = og[:, cols]
        o = o * lax.rsqrt(jnp.mean(o * o, axis=-1, keepdims=True) + EPS) * gain
        parts.append((o * _silu(gg[:, cols])).astype(BF16))
    mix_g = jnp.concatenate(parts, axis=-1)
    y = _mm_tn(mix_fT, wo_ref[0:W_FOX, :]) + _mm(mix_g, wo_ref[W_FOX:D_MODEL, :])
    z = DEEPNORM_ALPHA * x + gate_ref[...] * y
    mu = jnp.mean(z, axis=-1, keepdims=True)
    zc = z - mu
    var = jnp.mean(zc * zc, axis=-1, keepdims=True)
    y_ref[...] = zc * lax.rsqrt(var + EPS) * lng_ref[...] + lnb_ref[...]


def _merge(l, x, ofT, gfT, og, gg, gate, gain, wo, lng, lnb, tm, per_row_gate):
    B, S, _ = x.shape
    row = lambda n: pl.BlockSpec((None, tm, n), lambda b, i: (b, i, 0))
    colT = lambda n: pl.BlockSpec((None, n, tm), lambda b, i: (b, 0, i))
    gate_spec = row(D_MODEL) if per_row_gate else pl.BlockSpec((None, 1, D_MODEL), lambda b, i: (b, 0, 0))
    vec = lambda n: pl.BlockSpec((None, 1, n), lambda b, i: (l, 0, 0))
    return pl.pallas_call(
        _merge_kernel,
        grid=(B, S // tm),
        in_specs=[row(D_MODEL), colT(W_FOX), colT(W_FOX), row(WV_GLA), row(WV_GLA), gate_spec,
                  vec(DV_GLA), pl.BlockSpec((None, D_MODEL, D_MODEL), lambda b, i: (l, 0, 0)),
                  vec(D_MODEL), vec(D_MODEL)],
        out_specs=row(D_MODEL),
        out_shape=jax.ShapeDtypeStruct((B, S, D_MODEL), F32),
        compiler_params=_cparams(("arbitrary", "arbitrary")),
        name="merge",
    )(x, ofT, gfT, og, gg, gate, gain, wo, lng, lnb)


_ST_QF, _ST_KF, _ST_VF, _ST_GF, _ST_QG, _ST_KG, _ST_GL, _ST_FL = 0, 512, 1024, 1536, 2048, 2304, 2560, 2576
_ST_ROWS = 2592
_SR_VG, _SR_GG = 0, 512
_SR_COLS = 1024


def _inproj_sample_kernel(x_ref, shift_ref, scale_ref, wr_ref, wt_ref, bf_ref, wgkT_ref, bgk_ref,
                          qfT_ref, kfT_ref, vfT_ref, gfT_ref, logfT_ref, qgT_ref, kgT_ref, laT_ref,
                          vg_ref, gg_ref):
    h = _modulated_ln(x_ref[...], shift_ref[...], scale_ref[...]).astype(BF16)
    r = _mm(h, wr_ref[...])
    t = _mm_nt(wt_ref[...], h)
    qfT_ref[...] = t[_ST_QF:_ST_QF + W_FOX]
    kfT_ref[...] = t[_ST_KF:_ST_KF + W_FOX]
    vfT_ref[...] = t[_ST_VF:_ST_VF + W_FOX]
    gfT_ref[...] = t[_ST_GF:_ST_GF + W_FOX]
    qgT_ref[...] = t[_ST_QG:_ST_QG + WK_GLA]
    kgT_ref[...] = t[_ST_KG:_ST_KG + WK_GLA]
    logfT_ref[...] = _log_sigmoid(t[_ST_FL:_ST_FL + SUB_BF16] + bf_ref[...])[0:H_FOX]
    glT = t[_ST_GL:_ST_GL + GATE_RANK].astype(BF16)
    laT_ref[...] = _log_sigmoid(_mm(wgkT_ref[...], glT) + bgk_ref[...]) * (1.0 / GATE_NORM)
    vg_ref[...] = r[:, _SR_VG:_SR_VG + WV_GLA]
    gg_ref[...] = r[:, _SR_GG:_SR_GG + WV_GLA]


def _inproj_sample(l, x, shift, scale, W):
    n = x.shape[0]
    full = lambda a, b: pl.BlockSpec((a, b), lambda g: (0, 0))
    lay = lambda a, b: pl.BlockSpec((None, a, b), lambda g: (l, 0, 0))
    f = lambda a, b: jax.ShapeDtypeStruct((a, b), F32)
    return pl.pallas_call(
        _inproj_sample_kernel,
        grid=(1,),
        in_specs=[full(n, D_MODEL), full(n, D_MODEL), full(n, D_MODEL),
                  lay(D_MODEL, _SR_COLS), lay(_ST_ROWS, D_MODEL), lay(SUB_BF16, 1),
                  lay(WK_GLA, GATE_RANK), lay(WK_GLA, 1)],
        out_specs=[full(W_FOX, n), full(W_FOX, n), full(W_FOX, n), full(W_FOX, n), full(H_FOX, n),
                   full(WK_GLA, n), full(WK_GLA, n), full(WK_GLA, n),
                   full(n, WV_GLA), full(n, WV_GLA)],
        out_shape=[f(W_FOX, n), f(W_FOX, n), f(W_FOX, n), f(W_FOX, n), f(H_FOX, n),
                   f(WK_GLA, n), f(WK_GLA, n), f(WK_GLA, n),
                   f(n, WV_GLA), f(n, WV_GLA)],
        compiler_params=_cparams(("arbitrary",)),
        name="inproj_sample",
    )(x, shift, scale, W["wr_s"], W["wt_s"], W["bf_col"], W["wgkT"], W["bgk_col"])


GLA_SAMPLE_BATCH = 8


def _gla_sample_kernel(n_stacked, qT_ref, kT_ref, laT_ref, v_ref, s_ref, *refs):
    snew_ref, o_ref = refs[n_stacked:]
    nb = qT_ref.shape[1]
    bs = s_ref.shape[0]
    qT = qT_ref[...]
    kT = kT_ref[...]
    aT = jnp.exp(laT_ref[...])
    lane = lax.broadcasted_iota(jnp.int32, (1, nb), 1)
    for u in range(bs):
        onehot = lane == pl.program_id(0) * bs + u
        pick = lambda a: jnp.sum(jnp.where(onehot, a, 0.0), axis=1, keepdims=True)
        q_col, k_col, a_col = pick(qT), pick(kT), pick(aT)
        v = v_ref[u]
        for hh in range(H_GLA):
            rows = slice(hh * DK_GLA, (hh + 1) * DK_GLA)
            cols = slice(hh * DV_GLA, (hh + 1) * DV_GLA)
            st = a_col[rows] * s_ref[u, hh] + k_col[rows] * v[:, cols]
            snew_ref[u, hh] = st
            o_ref[u, :, cols] = jnp.sum(q_col[rows] * st, axis=0, keepdims=True)


def _gla_sample(l, qT, kT, laT, v, state, stacked):
    nb = qT.shape[1]
    bs = GLA_SAMPLE_BATCH
    small = pl.BlockSpec((WK_GLA, nb), lambda g: (0, 0))
    lay_state = pl.BlockSpec((None, bs, H_GLA, DK_GLA, DV_GLA), lambda g: (l, g, 0, 0, 0))
    in_specs = [small, small, small, pl.BlockSpec((bs, 1, WV_GLA), lambda g: (g, 0, 0)), lay_state]
    args = [qT, kT, laT, v.reshape(nb, 1, WV_GLA), state]
    aliases = {}
    if stacked is not None:
        aliases = {len(args): 0}
        in_specs = in_specs + [pl.BlockSpec(memory_space=pl.ANY)]
        args = args + [stacked]
    return pl.pallas_call(
        functools.partial(_gla_sample_kernel, len(aliases)),
        grid=(nb // bs,),
        in_specs=in_specs,
        out_specs=[lay_state, pl.BlockSpec((bs, 1, WV_GLA), lambda g: (g, 0, 0))],
        out_shape=[jax.ShapeDtypeStruct((DEPTH, nb, H_GLA, DK_GLA, DV_GLA), F32),
                   jax.ShapeDtypeStruct((nb, 1, WV_GLA), F32)],
        input_output_aliases=aliases,
        compiler_params=_cparams(("arbitrary",)),
        name="gla_sample",
    )(*args)


def _prep_weights(w_in, b_f, w_gk, b_gk):
    sl = lambda off, n: w_in[:, :, off:off + n]
    qf = sl(_QF, W_FOX) * FOX_SCALE
    qg = sl(_QG, WK_GLA) * GLA_SCALE
    kf, vf, fl, gf = sl(_KF, W_FOX), sl(_VF, W_FOX), sl(_FL, H_FOX), sl(_GF, W_FOX)
    kg, vg, gl, gg = sl(_KG, WK_GLA), sl(_VG, WV_GLA), sl(_GL, GATE_RANK), sl(_GG, WV_GLA)
    zc = lambda n: jnp.zeros((DEPTH, D_MODEL, n), F32)
    tr = lambda a: jnp.transpose(a, (0, 2, 1))
    fl_pad = jnp.concatenate([fl, zc(SUB_BF16 - H_FOX)], axis=2)
    glfl = jnp.concatenate([gl, fl, zc(LANES - GATE_RANK - H_FOX)], axis=2)
    wr_p = jnp.concatenate([kf, qg, kg, vg, gg, glfl], axis=2).astype(BF16)
    wt_p = tr(jnp.concatenate([qf, kf, vf, gf, fl_pad], axis=2)).astype(BF16)
    wr_s = jnp.concatenate([vg, gg], axis=2).astype(BF16)
    wt_s = tr(jnp.concatenate([qf, kf, vf, gf, qg, kg, gl, fl_pad], axis=2)).astype(BF16)
    zf = lambda n: jnp.zeros((DEPTH, n), F32)
    bf_col = jnp.concatenate([b_f, zf(SUB_BF16 - H_FOX)], axis=1)[:, :, None]
    bf_row = jnp.concatenate([zf(_FL_LANE), b_f, zf(LANES - _FL_LANE - H_FOX)], axis=1)[:, None, :]
    wgk_pad = jnp.concatenate([w_gk, jnp.zeros((DEPTH, LANES - GATE_RANK, WK_GLA), F32)], axis=1).astype(BF16)
    wgkT = tr(w_gk).astype(BF16)
    return dict(wr_p=wr_p, wt_p=wt_p, wr_s=wr_s, wt_s=wt_s, bf_col=bf_col, bf_row=bf_row,
                wgk_pad=wgk_pad, wgkT=wgkT, bgk_row=b_gk[:, None, :], bgk_col=b_gk[:, :, None])


def kernel(x_prompt, x_sample, cache_k, cache_v, cache_logf, state_gla, page_table, c_prompt, c_sample,
           w_ada, b_ada, w_in, b_f, w_gk, b_gk, gla_gain, w_out, ln_g, ln_b):
    B, S, _ = x_prompt.shape
    nb = x_sample.shape[0]
    W = _prep_weights(w_in, b_f, w_gk, b_gk)
    wo = w_out.astype(BF16)
    gain = gla_gain[:, None, :]
    lng = ln_g[:, None, :]
    lnb = ln_b[:, None, :]

    mod = _ada_all(jnp.concatenate([c_prompt, c_sample], axis=0), w_ada, b_ada)
    ck = jnp.transpose(cache_k, (0, 1, 3, 4, 2))
    cv = jnp.transpose(cache_v, (0, 1, 3, 4, 2))
    cf = jnp.transpose(cache_logf, (0, 1, 3, 2))

    xp = x_prompt
    xs = x_sample.reshape(1, nb, D_MODEL)
    sp_l, ks_l, vs_l, fs_l = ([] for _ in range(4))
    stacked_p = None
    stacked_s = None
    for l in range(DEPTH):
        shift_p = mod[l, :B, None, 0:D_MODEL]
        scale_p = mod[l, :B, None, D_MODEL:2 * D_MODEL]
        gate_p = mod[l, :B, None, 2 * D_MODEL:]
        shift_s = mod[l, B:, 0:D_MODEL]
        scale_s = mod[l, B:, D_MODEL:2 * D_MODEL]
        gate_s = mod[l, B:, 2 * D_MODEL:][None]

        (qT, ka, vTb, kT_all, vT_all, lfT_all, gfT, qg, kg, vg, la, gg) = _inproj_prompt(
            l, xp, shift_p, scale_p, W, stacked_p)
        stacked_p = (kT_all, vT_all, lfT_all)
        (qfT, kfT, vfT, gfTs, lfT, qgT, kgT, laT, vgs, ggs) = _inproj_sample(l, xs[0], shift_s, scale_s, W)
        o_fT = _fox_prompt(qT, ka, vTb)
        o_g, s_p, o_fTs = _gla_decode(l, qg, kg, vg, la, page_table, qfT, kfT, vfT, lfT, ck, cv, cf)
        xp = _merge(l, xp, o_fT, gfT, o_g, gg, gate_p, gain, wo, lng, lnb, tm=256, per_row_gate=False)
        sp_l.append(s_p)

        stacked_s, o_gs = _gla_sample(l, qgT, kgT, laT, vgs, state_gla, stacked_s)
        xs = _merge(l, xs, o_fTs[None], gfTs[None], o_gs.reshape(1, nb, WV_GLA), ggs[None], gate_s,
                    gain, wo, lng, lnb, tm=nb, per_row_gate=True)
        ks_l.append(kfT); vs_l.append(vfT); fs_l.append(lfT)

    kp = jnp.transpose(stacked_p[0].reshape(DEPTH, B, H_FOX, HD_FOX, S), (0, 1, 4, 2, 3))
    vp = jnp.transpose(stacked_p[1].reshape(DEPTH, B, H_FOX, HD_FOX, S), (0, 1, 4, 2, 3))
    fp = jnp.transpose(stacked_p[2], (0, 1, 3, 2))
    ks = jnp.transpose(jnp.stack(ks_l).reshape(DEPTH, H_FOX, HD_FOX, nb), (0, 3, 1, 2))[:, :, None]
    vs = jnp.transpose(jnp.stack(vs_l).reshape(DEPTH, H_FOX, HD_FOX, nb), (0, 3, 1, 2))[:, :, None]
    fs = jnp.transpose(jnp.stack(fs_l), (0, 2, 1))[:, :, None]
    return (xp, xs.reshape(nb, 1, D_MODEL), kp, vp, fp, jnp.stack(sp_l), ks, vs, fs, stacked_s)
```

```python
import functools

import jax
import jax.numpy as jnp
import numpy as np
from jax import lax
from jax.experimental import pallas as pl
from jax.experimental.pallas import tpu as pltpu

D_MODEL = 1024
DEPTH = 4
PAGE_SIZE = 128
HD_FOX = 64
W_FOX = D_MODEL // 2
H_FOX = W_FOX // HD_FOX
H_GLA = 4
DV_GLA = (D_MODEL // 2) // H_GLA
DK_GLA = DV_GLA // 2
WK_GLA = H_GLA * DK_GLA
WV_GLA = H_GLA * DV_GLA
GATE_RANK = 16
GATE_NORM = 16.0
EPS = 1e-5
DEEPNORM_ALPHA = (2.0 * DEPTH) ** 0.25
_SPLITS = (W_FOX, W_FOX, W_FOX, H_FOX, W_FOX, WK_GLA, WK_GLA, WV_GLA, GATE_RANK, WV_GLA)
_OFF = [0]
for _s in _SPLITS:
    _OFF.append(_OFF[-1] + _s)
(_QF, _KF, _VF, _FL, _GF, _QG, _KG, _VG, _GL, _GG) = _OFF[:-1]

LANES = 128
SUB_BF16 = 16
FOX_SCALE = HD_FOX ** -0.5
GLA_SCALE = DK_GLA ** -0.5
GLA_SUB = 8
GLA_CHUNK = 128
FOX_TILE = 512
VMEM_LIMIT = 56 * 1024 * 1024

F32 = jnp.float32
BF16 = jnp.bfloat16


def _cparams(sem):
    return pltpu.CompilerParams(dimension_semantics=sem, vmem_limit_bytes=VMEM_LIMIT)


def _split3(x):
    hi = x.astype(BF16).astype(F32)
    r = x - hi
    mid = r.astype(BF16).astype(F32)
    lo = (r - mid).astype(BF16).astype(F32)
    return hi, mid, lo


def _exact_prefix_mm(tri, x, left):
    hi, mid, lo = _split3(x)
    if left:
        return _mm(tri, hi.astype(BF16)) + _mm(tri, mid.astype(BF16)) + _mm(tri, lo.astype(BF16))
    return _mm(hi.astype(BF16), tri) + _mm(mid.astype(BF16), tri) + _mm(lo.astype(BF16), tri)


def _log_sigmoid(x):
    return -(jnp.maximum(-x, 0.0) + jnp.log1p(jnp.exp(-jnp.abs(x))))


def _silu(x):
    return x * (1.0 / (1.0 + jnp.exp(-x)))


def _mm(a, b):
    return jnp.dot(a, b, preferred_element_type=F32)


def _mm_nt(a, b):
    return lax.dot_general(a, b, (((1,), (1,)), ((), ())), preferred_element_type=F32)


def _mm_tn(a, b):
    return lax.dot_general(a, b, (((0,), (0,)), ((), ())), preferred_element_type=F32)


def _ada_kernel(c_ref, w_ref, b_ref, o_ref):
    c = c_ref[...]
    o_ref[...] = _mm(_silu(c).astype(BF16), w_ref[...].astype(BF16)) + b_ref[...]


def _ada_all(c_all, w_ada, b_ada):
    n = c_all.shape[0]
    return pl.pallas_call(
        _ada_kernel,
        grid=(DEPTH, 3),
        in_specs=[pl.BlockSpec((n, D_MODEL), lambda l, j: (0, 0)),
                  pl.BlockSpec((None, D_MODEL, D_MODEL), lambda l, j: (l, 0, j)),
                  pl.BlockSpec((None, 1, D_MODEL), lambda l, j: (l, 0, j))],
        out_specs=pl.BlockSpec((None, n, D_MODEL), lambda l, j: (l, 0, j)),
        out_shape=jax.ShapeDtypeStruct((DEPTH, n, 3 * D_MODEL), F32),
        compiler_params=_cparams(("arbitrary", "arbitrary")),
        name="ada_mod",
    )(c_all, w_ada, b_ada.reshape(DEPTH, 1, 3 * D_MODEL))


def _modulated_ln(x, shift, scale):
    mu = jnp.mean(x, axis=-1, keepdims=True)
    xc = x - mu
    var = jnp.mean(xc * xc, axis=-1, keepdims=True)
    return (xc * lax.rsqrt(var + EPS)) * (1.0 + scale) + shift


_R_KF, _R_QG, _R_KG, _R_VG, _R_GG, _R_GLFL = 0, 512, 768, 1024, 1536, 2048
_R_COLS = 2176
_FL_LANE = GATE_RANK
_T_QF, _T_KF, _T_VF, _T_GF, _T_FL = 0, 512, 1024, 1536, 2048
_T_ROWS = 2048 + SUB_BF16


def _inproj_prompt_kernel(n_stacked, x_ref, shift_ref, scale_ref, wr_ref, wt_ref, bfc_ref, bfr_ref, wgk_ref,
                          bgk_ref, *refs):
    (qT_ref, ka_ref, vTb_ref, kT_ref, vT_ref, logfT_ref, gfT_ref,
     qg_ref, kg_ref, vg_ref, la_ref, gg_ref, carry_ref) = refs[n_stacked:]
    i = pl.program_id(1)
    tm = x_ref.shape[0]
    h = _modulated_ln(x_ref[...], shift_ref[...], scale_ref[...]).astype(BF16)
    r = _mm_nt(h, wr_ref[...])
    t = _mm_nt(wt_ref[...], h)

    kT = t[_T_KF:_T_KF + W_FOX]
    vT = t[_T_VF:_T_VF + W_FOX]
    kT_ref[...] = kT
    vT_ref[...] = vT
    vTb_ref[...] = vT.astype(BF16)
    gfT_ref[...] = t[_T_GF:_T_GF + W_FOX].astype(gfT_ref.dtype)
    logfT_ref[...] = _log_sigmoid(t[_T_FL:_T_FL + SUB_BF16] + bfc_ref[...])[0:H_FOX]

    sub = lax.broadcasted_iota(jnp.int32, (SUB_BF16, tm), 0)
    ones3 = jnp.where(sub < 3, 1.0, 0.0).astype(BF16)
    zeros = jnp.zeros((LANES - HD_FOX - SUB_BF16, tm), BF16)
    for hh in range(H_FOX):
        qT_ref[hh, 0:HD_FOX, :] = t[_T_QF + hh * HD_FOX:_T_QF + (hh + 1) * HD_FOX].astype(BF16)
        qT_ref[hh, HD_FOX:HD_FOX + SUB_BF16, :] = ones3
        qT_ref[hh, HD_FOX + SUB_BF16:LANES, :] = zeros

    @pl.when(i == 0)
    def _():
        carry_ref[...] = jnp.zeros_like(carry_ref)

    glfl = r[:, _R_GLFL:_R_GLFL + LANES]
    logf_r = _log_sigmoid(glfl + bfr_ref[...])
    row = lax.broadcasted_iota(jnp.int32, (tm, tm), 0)
    col = lax.broadcasted_iota(jnp.int32, (tm, tm), 1)
    lower = jnp.where(col <= row, 1.0, 0.0).astype(BF16)
    carry = carry_ref[0:1, :]
    cum = _exact_prefix_mm(lower, logf_r, left=True) + carry
    carry_ref[...] = jnp.broadcast_to(carry + jnp.sum(logf_r, axis=0, keepdims=True), carry_ref.shape)
    chi, cmid, clo = _split3(cum)

    lane = lax.broadcasted_iota(jnp.int32, (tm, LANES), 1)
    for hh in range(H_FOX):
        blk = r[:, _R_KF + (hh // 2) * LANES:_R_KF + (hh // 2 + 1) * LANES]
        if hh % 2:
            blk = pltpu.roll(blk, HD_FOX, axis=1)
        src = _FL_LANE + hh
        e0 = pltpu.roll(chi, HD_FOX - src, axis=1)
        e1 = pltpu.roll(cmid, HD_FOX + 1 - src, axis=1)
        e2 = pltpu.roll(clo, HD_FOX + 2 - src, axis=1)
        extra = jnp.where(lane == HD_FOX, -e0,
                          jnp.where(lane == HD_FOX + 1, -e1, jnp.where(lane == HD_FOX + 2, -e2, 0.0)))
        ka_ref[hh] = jnp.where(lane < HD_FOX, blk, extra).astype(BF16)

    qg_ref[...] = r[:, _R_QG:_R_QG + WK_GLA]
    kg_ref[...] = r[:, _R_KG:_R_KG + WK_GLA]
    vg_ref[...] = r[:, _R_VG:_R_VG + WV_GLA].astype(BF16)
    gg_ref[...] = r[:, _R_GG:_R_GG + WV_GLA].astype(gg_ref.dtype)
    la_ref[...] = _log_sigmoid(_mm(glfl.astype(BF16), wgk_ref[...]) + bgk_ref[...]) * (1.0 / GATE_NORM)


def _inproj_prompt(l, x, shift, scale, W, stacked, tm=512):
    B, S, _ = x.shape
    nt = S // tm
    row = lambda n: pl.BlockSpec((None, tm, n), lambda b, i: (b, i, 0))
    colT = lambda n: pl.BlockSpec((None, n, tm), lambda b, i: (b, 0, i))
    layT = lambda n: pl.BlockSpec((None, None, n, tm), lambda b, i: (l, b, 0, i))
    lay = lambda a, c: pl.BlockSpec((None, a, c), lambda b, i: (l, 0, 0))
    out_shape = [
        jax.ShapeDtypeStruct((B, H_FOX, LANES, S), BF16),
        jax.ShapeDtypeStruct((B, H_FOX, S, LANES), BF16),
        jax.ShapeDtypeStruct((B, W_FOX, S), BF16),
        jax.ShapeDtypeStruct((DEPTH, B, W_FOX, S), F32),
        jax.ShapeDtypeStruct((DEPTH, B, W_FOX, S), F32),
        jax.ShapeDtypeStruct((DEPTH, B, H_FOX, S), F32),
        jax.ShapeDtypeStruct((B, W_FOX, S), BF16),
        jax.ShapeDtypeStruct((B, S, WK_GLA), F32),
        jax.ShapeDtypeStruct((B, S, WK_GLA), F32),
        jax.ShapeDtypeStruct((B, S, WV_GLA), BF16),
        jax.ShapeDtypeStruct((B, S, WK_GLA), F32),
        jax.ShapeDtypeStruct((B, S, WV_GLA), BF16),
    ]
    out_specs = [
        pl.BlockSpec((None, H_FOX, LANES, tm), lambda b, i: (b, 0, 0, i)),
        pl.BlockSpec((None, H_FOX, tm, LANES), lambda b, i: (b, 0, i, 0)),
        colT(W_FOX), layT(W_FOX), layT(W_FOX), layT(H_FOX), colT(W_FOX),
        row(WK_GLA), row(WK_GLA), row(WV_GLA), row(WK_GLA), row(WV_GLA),
    ]
    in_specs = [
        pl.BlockSpec((None, tm, D_MODEL), lambda b, i: (b, i, 0)),
        pl.BlockSpec((None, 1, D_MODEL), lambda b, i: (b, 0, 0)),
        pl.BlockSpec((None, 1, D_MODEL), lambda b, i: (b, 0, 0)),
        lay(_R_COLS, D_MODEL), lay(_T_ROWS, D_MODEL), lay(SUB_BF16, 1), lay(1, LANES),
        lay(LANES, WK_GLA), lay(1, WK_GLA),
    ]
    args = [x, shift, scale, W["wr_p"], W["wt_p"], W["bf_col"], W["bf_row"], W["wgk_pad"], W["bgk_row"]]
    aliases = {}
    if stacked is not None:
        aliases = {len(args) + n: 3 + n for n in range(len(stacked))}
        in_specs = in_specs + [pl.BlockSpec(memory_space=pl.ANY)] * len(stacked)
        args = args + list(stacked)
    return pl.pallas_call(
        functools.partial(_inproj_prompt_kernel, len(aliases)),
        grid=(B, nt),
        in_specs=in_specs,
        out_specs=out_specs,
        out_shape=out_shape,
        input_output_aliases=aliases,
        scratch_shapes=[pltpu.VMEM((8, LANES), F32)],
        compiler_params=_cparams(("arbitrary", "arbitrary")),
        name="inproj_prompt",
    )(*args)


def _fox_prompt_kernel(it_ref, jt_ref, ka_ref, qT_ref, vT_ref, oT_ref, m_ref, l_ref, acc_ref):
    pidx = pl.program_id(1)
    i = it_ref[pidx]
    j = jt_ref[pidx]
    tk = ka_ref.shape[1]
    tq = qT_ref.shape[2]

    @pl.when(j == 0)
    def _():
        m_ref[...] = jnp.full_like(m_ref, -jnp.inf)
        l_ref[...] = jnp.zeros_like(l_ref)
        acc_ref[...] = jnp.zeros_like(acc_ref)

    def update(hh, q0, nq, nk, masked):
        cols = slice(q0, q0 + nq)
        rows = slice(hh * HD_FOX, (hh + 1) * HD_FOX)
        sT = _mm(ka_ref[hh, 0:nk, :], qT_ref[hh, :, cols])
        if masked:
            kpos = lax.broadcasted_iota(jnp.int32, (nk, nq), 0)
            qpos = lax.broadcasted_iota(jnp.int32, (nk, nq), 1) + q0
            sT = jnp.where(kpos <= qpos, sT, -jnp.inf)
        m_prev = m_ref[hh, :, cols]
        m_new = jnp.maximum(m_prev, jnp.max(sT, axis=0, keepdims=True))
        alpha = jnp.exp(m_prev - m_new)
        pT = jnp.exp(sT - m_new)
        l_ref[hh, :, cols] = alpha * l_ref[hh, :, cols] + jnp.sum(pT, axis=0, keepdims=True)
        m_ref[hh, :, cols] = m_new
        acc_ref[rows, cols] = acc_ref[rows, cols] * alpha + _mm(vT_ref[rows, 0:nk], pT.astype(BF16))

    def run(masked):
        for hh in range(H_FOX):
            update(hh, 0, tq, tk, masked)

    @pl.when(j < i)
    def _():
        run(False)

    @pl.when(j == i)
    def _():
        run(True)
        for hh in range(H_FOX):
            rows = slice(hh * HD_FOX, (hh + 1) * HD_FOX)
            oT_ref[rows, :] = (acc_ref[rows, :] * (1.0 / l_ref[hh])).astype(oT_ref.dtype)


def _fox_prompt(qT, ka, vTb):
    B, _, S, _ = ka.shape
    t = min(FOX_TILE, S)
    n = S // t
    pairs = [(i, j) for i in range(n) for j in range(i + 1)]
    it = jnp.asarray(np.array([p[0] for p in pairs], np.int32))
    jt = jnp.asarray(np.array([p[1] for p in pairs], np.int32))
    grid_spec = pltpu.PrefetchScalarGridSpec(
        num_scalar_prefetch=2, grid=(B, len(pairs)),
        in_specs=[pl.BlockSpec((None, H_FOX, t, LANES), lambda b, p, it, jt: (b, 0, jt[p], 0)),
                  pl.BlockSpec((None, H_FOX, LANES, t), lambda b, p, it, jt: (b, 0, 0, it[p])),
                  pl.BlockSpec((None, W_FOX, t), lambda b, p, it, jt: (b, 0, jt[p]))],
        out_specs=pl.BlockSpec((None, W_FOX, t), lambda b, p, it, jt: (b, 0, it[p])),
        scratch_shapes=[pltpu.VMEM((H_FOX, 1, t), F32), pltpu.VMEM((H_FOX, 1, t), F32),
                        pltpu.VMEM((W_FOX, t), F32)])
    return pl.pallas_call(
        _fox_prompt_kernel,
        grid_spec=grid_spec,
        out_shape=jax.ShapeDtypeStruct((B, W_FOX, S), BF16),
        compiler_params=_cparams(("arbitrary", "arbitrary")),
        name="fox_prompt",
    )(it, jt, ka, qT, vTb)


def _decode_one(sidx, qT_ref, kT_ref, vT_ref, lfT_ref, k_pages, v_pages, f_pages, oT_ref):
    n_pages = len(k_pages)
    nb = qT_ref.shape[1]
    P = PAGE_SIZE
    R = n_pages * H_FOX

    onehot = lax.broadcasted_iota(jnp.int32, (1, nb), 1) == sidx
    pick = lambda ref: jnp.sum(jnp.where(onehot, ref[...], 0.0), axis=1, keepdims=True)
    q_col = pick(qT_ref)
    k_col = pick(kT_ref)
    v_col = pick(vT_ref)
    lf_new = pick(lfT_ref)
    q_b = [jnp.broadcast_to(q_col[hh * HD_FOX:(hh + 1) * HD_FOX], (HD_FOX, P)) for hh in range(H_FOX)]

    sub8 = lax.broadcasted_iota(jnp.int32, (H_FOX, P), 0)
    s_pages = []
    for g in range(n_pages):
        s = jnp.zeros((H_FOX, P), F32)
        for hh in range(H_FOX):
            s = jnp.where(sub8 == hh, jnp.sum(k_pages[g][hh] * q_b[hh], axis=0, keepdims=True), s)
        s_pages.append(s)
    s_all = jnp.concatenate(s_pages, axis=0)
    lf_all = jnp.concatenate([f_pages[g][...] for g in range(n_pages)], axis=0)

    upper = jnp.where(lax.broadcasted_iota(jnp.int32, (P, P), 0) <= lax.broadcasted_iota(jnp.int32, (P, P), 1),
                      1.0, 0.0).astype(BF16)
    within = _exact_prefix_mm(upper, lf_all, left=False)
    tot = jnp.sum(lf_all, axis=1, keepdims=True)
    rr = lax.broadcasted_iota(jnp.int32, (R, R), 0)
    cc = lax.broadcasted_iota(jnp.int32, (R, R), 1)
    earlier = jnp.where(((rr % H_FOX) == (cc % H_FOX)) & (cc < rr), 1.0, 0.0).astype(BF16)
    before = _exact_prefix_mm(earlier, jnp.broadcast_to(tot, (R, P)), left=True)
    x_all = s_all - (within + before)
    total = before[R - H_FOX:R, 0:1] + tot[R - H_FOX:R]

    qk = q_col * k_col
    s_self = jnp.concatenate(
        [jnp.sum(qk[hh * HD_FOX:(hh + 1) * HD_FOX], axis=0, keepdims=True) for hh in range(H_FOX)], axis=0)
    x_self = s_self - (total + lf_new)

    rowmax = jnp.max(x_all, axis=1, keepdims=True)
    m = x_self
    for g in range(n_pages):
        m = jnp.maximum(m, rowmax[g * H_FOX:(g + 1) * H_FOX])
    p_all = jnp.exp(x_all - jnp.concatenate([m] * n_pages, axis=0))
    p_self = jnp.exp(x_self - m)
    rowsum = jnp.sum(p_all, axis=1, keepdims=True)
    l = p_self
    for g in range(n_pages):
        l = l + rowsum[g * H_FOX:(g + 1) * H_FOX]
    inv_l = 1.0 / l

    outs = []
    for hh in range(H_FOX):
        acc = v_pages[0][hh] * p_all[hh:hh + 1]
        for g in range(1, n_pages):
            acc = acc + v_pages[g][hh] * p_all[g * H_FOX + hh:g * H_FOX + hh + 1]
        o = jnp.sum(acc, axis=1, keepdims=True) + p_self[hh:hh + 1] * v_col[hh * HD_FOX:(hh + 1) * HD_FOX]
        outs.append(o * inv_l[hh:hh + 1])
    o_col = jnp.concatenate(outs, axis=0)
    oT_ref[...] = jnp.where(onehot, o_col, oT_ref[...])


def _gla_decode_kernel(pt_ref, q_ref, k_ref, v_ref, la_ref, sel_ref, til_ref,
                       qTs_ref, kTs_ref, vTs_ref, lfTs_ref, *refs):
    n_pages = (len(refs) - 6) // 3
    k_pages = refs[0:n_pages]
    v_pages = refs[n_pages:2 * n_pages]
    f_pages = refs[2 * n_pages:3 * n_pages]
    o_ref, sfin_ref, oTs_ref, b_scr, k_scr, state_ref = refs[3 * n_pages:]
    c = pl.program_id(1)
    L = q_ref.shape[0]
    nsub = L // GLA_SUB
    nb = qTs_ref.shape[1]
    sidx = pl.program_id(0) * pl.num_programs(1) + c

    @pl.when(sidx == 0)
    def _():
        oTs_ref[...] = jnp.zeros_like(oTs_ref)

    @pl.when(c == 0)
    def _():
        state_ref[...] = jnp.zeros_like(state_ref)

    _decode_one(jnp.minimum(sidx, nb - 1), qTs_ref, kTs_ref, vTs_ref, lfTs_ref, k_pages, v_pages, f_pages, oTs_ref)

    q = q_ref[...]
    k = k_ref[...]

    r_i = lax.broadcasted_iota(jnp.int32, (L, L), 0)
    c_i = lax.broadcasted_iota(jnp.int32, (L, L), 1)
    lower = jnp.where(c_i <= r_i, 1.0, 0.0).astype(BF16)
    b = _exact_prefix_mm(lower, la_ref[...], left=True)
    b_scr[...] = b
    k_scr[...] = k

    head_of_lane = lax.broadcasted_iota(jnp.int32, (L, WK_GLA), 1) // DK_GLA

    def rows_bcast(ref, first, stride, n, reps):
        return jnp.concatenate(
            [jnp.broadcast_to(ref[first + stride * s:first + stride * s + 1, :], (reps, WK_GLA)) for s in range(n)],
            axis=0)

    a8 = jnp.zeros((L, H_GLA * GLA_SUB), F32)
    for j in range(GLA_SUB):
        bj = rows_bcast(b_scr, j, GLA_SUB, nsub, GLA_SUB)
        kj = rows_bcast(k_scr, j, GLA_SUB, nsub, GLA_SUB)
        p = q * kj * jnp.exp(jnp.minimum(b - bj, 0.0))
        a8 = a8 + _mm(p.astype(BF16), sel_ref[j])
    a_exp = _mm(a8.astype(BF16), til_ref[...])
    diag_mask = ((r_i // GLA_SUB) == (c_i // GLA_SUB)) & ((c_i % GLA_SUB) <= (r_i % GLA_SUB))
    a_heads = [jnp.where(diag_mask, a_exp[:, hh * L:(hh + 1) * L], 0.0) for hh in range(H_GLA)]

    m = GLA_SUB
    while m < L:
        bref = rows_bcast(b_scr, m - 1, 2 * m, L // (2 * m), 2 * m)
        e = jnp.exp(-jnp.abs(b - bref))
        qe = q * e
        ke = (k * e).astype(BF16)
        qs = jnp.concatenate([jnp.where(head_of_lane == hh, qe, 0.0).astype(BF16) for hh in range(H_GLA)], axis=0)
        prod = _mm_nt(qs, ke)
        mask = ((r_i // (2 * m)) == (c_i // (2 * m))) & ((r_i % (2 * m)) >= m) & ((c_i % (2 * m)) < m)
        a_heads = [jnp.where(mask, prod[hh * L:(hh + 1) * L], a_heads[hh]) for hh in range(H_GLA)]
        m *= 2

    state = state_ref[...]
    o_inter = _mm((q * jnp.exp(b)).astype(BF16), state.astype(BF16))
    vb = v_ref[...].astype(BF16)
    for hh in range(H_GLA):
        cols = slice(hh * DV_GLA, (hh + 1) * DV_GLA)
        o_ref[:, cols] = (_mm(a_heads[hh].astype(BF16), vb[:, cols]) + o_inter[:, cols]).astype(o_ref.dtype)

    b_last = b_scr[L - 1:L, :]
    upd = _mm_tn((k * jnp.exp(b_last - b)).astype(BF16), vb)
    rr = lax.broadcasted_iota(jnp.int32, (WK_GLA, WV_GLA), 0) // DK_GLA
    cc = lax.broadcasted_iota(jnp.int32, (WK_GLA, WV_GLA), 1) // DV_GLA
    eye = (lax.broadcasted_iota(jnp.int32, (WK_GLA, WK_GLA), 0)
           == lax.broadcasted_iota(jnp.int32, (WK_GLA, WK_GLA), 1))
    dec_col = jnp.sum(jnp.where(eye, jnp.broadcast_to(jnp.exp(b_last), (WK_GLA, WK_GLA)), 0.0),
                      axis=1, keepdims=True)
    new_state = jnp.where(rr == cc, dec_col * state + upd, 0.0)
    state_ref[...] = new_state

    @pl.when(c == pl.num_programs(1) - 1)
    def _():
        for hh in range(H_GLA):
            sfin_ref[hh] = new_state[hh * DK_GLA:(hh + 1) * DK_GLA, hh * DV_GLA:(hh + 1) * DV_GLA]


def _gla_consts():
    L = GLA_CHUNK
    n = H_GLA * GLA_SUB
    hk = np.arange(WK_GLA) // DK_GLA
    cols = np.arange(n)
    j = np.arange(GLA_SUB)
    sel = ((hk[None, :, None] == (cols // GLA_SUB)[None, None, :])
           & (j[:, None, None] == (cols % GLA_SUB)[None, None, :]))
    out = np.arange(H_GLA * L)
    til = (((cols // GLA_SUB)[:, None] == (out // L)[None, :])
           & ((cols % GLA_SUB)[:, None] == ((out % L) % GLA_SUB)[None, :]))
    return jnp.asarray(sel, BF16), jnp.asarray(til, BF16)


def _gla_decode(l, q, k, v, la, page_table, qTs, kTs, vTs, lfTs, ck, cv, cf):
    B, S, _ = q.shape
    L = GLA_CHUNK
    nc = S // L
    nb, n_pages = page_table.shape
    assert nb <= B * nc, "one sample sequence per grid step"
    sel, til = _gla_consts()
    seq = lambda b, c: jnp.minimum(b * nc + c, nb - 1)
    row = lambda n: pl.BlockSpec((None, L, n), lambda b, c, pt: (b, c, 0))
    small = lambda a: pl.BlockSpec((a, nb), lambda b, c, pt: (0, 0))
    kv_spec = lambda g: pl.BlockSpec((None, None, H_FOX, HD_FOX, PAGE_SIZE),
                                     lambda b, c, pt: (l, pt[seq(b, c), g], 0, 0, 0))
    f_spec = lambda g: pl.BlockSpec((None, None, H_FOX, PAGE_SIZE), lambda b, c, pt: (l, pt[seq(b, c), g], 0, 0))
    in_specs = ([row(WK_GLA), row(WK_GLA), row(WV_GLA), row(WK_GLA),
                 pl.BlockSpec(sel.shape, lambda b, c, pt: (0, 0, 0)),
                 pl.BlockSpec(til.shape, lambda b, c, pt: (0, 0)),
                 small(W_FOX), small(W_FOX), small(W_FOX), small(H_FOX)]
                + [kv_spec(g) for g in range(n_pages)]
                + [kv_spec(g) for g in range(n_pages)]
                + [f_spec(g) for g in range(n_pages)])
    grid_spec = pltpu.PrefetchScalarGridSpec(
        num_scalar_prefetch=1, grid=(B, nc), in_specs=in_specs,
        out_specs=[row(WV_GLA),
                   pl.BlockSpec((None, H_GLA, DK_GLA, DV_GLA), lambda b, c, pt: (b, 0, 0, 0)),
                   pl.BlockSpec((W_FOX, nb), lambda b, c, pt: (0, 0))],
        scratch_shapes=[pltpu.VMEM((L, WK_GLA), F32), pltpu.VMEM((L, WK_GLA), F32),
                        pltpu.VMEM((WK_GLA, WV_GLA), F32)])
    return pl.pallas_call(
        _gla_decode_kernel,
        grid_spec=grid_spec,
        out_shape=[jax.ShapeDtypeStruct((B, S, WV_GLA), BF16),
                   jax.ShapeDtypeStruct((B, H_GLA, DK_GLA, DV_GLA), F32),
                   jax.ShapeDtypeStruct((W_FOX, nb), F32)],
        compiler_params=_cparams(("arbitrary", "arbitrary")),
        name="gla_decode",
    )(page_table, q, k, v, la, sel, til, qTs, kTs, vTs, lfTs,
      *([ck] * n_pages), *([cv] * n_pages), *([cf] * n_pages))


def _merge_kernel(x_ref, ofT_ref, gfT_ref, og_ref, gg_ref, gate_ref, gain_ref, wo_ref, lng_ref, lnb_ref, y_ref):
    x = x_ref[...]
    mix_fT = (ofT_ref[...].astype(F32) * _silu(gfT_ref[...].astype(F32))).astype(BF16)
    og = og_ref[...].astype(F32)
    gg = gg_ref[...].astype(F32)
    gain = gain_ref[...]
    parts = []
    for hh in range(H_GLA):
        cols = slice(hh * DV_GLA, (hh + 1) * DV_GLA)
        o = og[:, cols]
        o = o * lax.rsqrt(jnp.mean(o * o, axis=-1, keepdims=True) + EPS) * gain
        parts.append((o * _silu(gg[:, cols])).astype(BF16))
    mix_g = jnp.concatenate(parts, axis=-1)
    y = _mm_tn(mix_fT, wo_ref[0:W_FOX, :]) + _mm(mix_g, wo_ref[W_FOX:D_MODEL, :])
    z = DEEPNORM_ALPHA * x + gate_ref[...] * y
    mu = jnp.mean(z, axis=-1, keepdims=True)
    zc = z - mu
    var = jnp.mean(zc * zc, axis=-1, keepdims=True)
    y_ref[...] = zc * lax.rsqrt(var + EPS) * lng_ref[...] + lnb_ref[...]


def _merge(l, x, ofT, gfT, og, gg, gate, gain, wo, lng, lnb, tm, per_row_gate):
    B, S, _ = x.shape
    row = lambda n: pl.BlockSpec((None, tm, n), lambda b, i: (b, i, 0))
    colT = lambda n: pl.BlockSpec((None, n, tm), lambda b, i: (b, 0, i))
    gate_spec = row(D_MODEL) if per_row_gate else pl.BlockSpec((None, 1, D_MODEL), lambda b, i: (b, 0, 0))
    vec = lambda n: pl.BlockSpec((None, 1, n), lambda b, i: (l, 0, 0))
    return pl.pallas_call(
        _merge_kernel,
        grid=(B, S // tm),
        in_specs=[row(D_MODEL), colT(W_FOX), colT(W_FOX), row(WV_GLA), row(WV_GLA), gate_spec,
                  vec(DV_GLA), pl.BlockSpec((None, D_MODEL, D_MODEL), lambda b, i: (l, 0, 0)),
                  vec(D_MODEL), vec(D_MODEL)],
        out_specs=row(D_MODEL),
        out_shape=jax.ShapeDtypeStruct((B, S, D_MODEL), F32),
        compiler_params=_cparams(("arbitrary", "arbitrary")),
        name="merge",
    )(x, ofT, gfT, og, gg, gate, gain, wo, lng, lnb)


_ST_QF, _ST_KF, _ST_VF, _ST_GF, _ST_QG, _ST_KG, _ST_GL, _ST_FL = 0, 512, 1024, 1536, 2048, 2304, 2560, 2576
_ST_ROWS = 2592
_SR_VG, _SR_GG = 0, 512
_SR_COLS = 1024


def _inproj_sample_kernel(x_ref, shift_ref, scale_ref, wr_ref, wt_ref, bf_ref, wgkT_ref, bgk_ref,
                          qfT_ref, kfT_ref, vfT_ref, gfT_ref, logfT_ref, qgT_ref, kgT_ref, laT_ref,
                          vg_ref, gg_ref):
    h = _modulated_ln(x_ref[...], shift_ref[...], scale_ref[...]).astype(BF16)
    r = _mm(h, wr_ref[...])
    t = _mm_nt(wt_ref[...], h)
    qfT_ref[...] = t[_ST_QF:_ST_QF + W_FOX]
    kfT_ref[...] = t[_ST_KF:_ST_KF + W_FOX]
    vfT_ref[...] = t[_ST_VF:_ST_VF + W_FOX]
    gfT_ref[...] = t[_ST_GF:_ST_GF + W_FOX]
    qgT_ref[...] = t[_ST_QG:_ST_QG + WK_GLA]
    kgT_ref[...] = t[_ST_KG:_ST_KG + WK_GLA]
    logfT_ref[...] = _log_sigmoid(t[_ST_FL:_ST_FL + SUB_BF16] + bf_ref[...])[0:H_FOX]
    glT = t[_ST_GL:_ST_GL + GATE_RANK].astype(BF16)
    laT_ref[...] = _log_sigmoid(_mm(wgkT_ref[...], glT) + bgk_ref[...]) * (1.0 / GATE_NORM)
    vg_ref[...] = r[:, _SR_VG:_SR_VG + WV_GLA]
    gg_ref[...] = r[:, _SR_GG:_SR_GG + WV_GLA]


def _inproj_sample(l, x, shift, scale, W):
    n = x.shape[0]
    full = lambda a, b: pl.BlockSpec((a, b), lambda g: (0, 0))
    lay = lambda a, b: pl.BlockSpec((None, a, b), lambda g: (l, 0, 0))
    f = lambda a, b: jax.ShapeDtypeStruct((a, b), F32)
    return pl.pallas_call(
        _inproj_sample_kernel,
        grid=(1,),
        in_specs=[full(n, D_MODEL), full(n, D_MODEL), full(n, D_MODEL),
                  lay(D_MODEL, _SR_COLS), lay(_ST_ROWS, D_MODEL), lay(SUB_BF16, 1),
                  lay(WK_GLA, GATE_RANK), lay(WK_GLA, 1)],
        out_specs=[full(W_FOX, n), full(W_FOX, n), full(W_FOX, n), full(W_FOX, n), full(H_FOX, n),
                   full(WK_GLA, n), full(WK_GLA, n), full(WK_GLA, n),
                   full(n, WV_GLA), full(n, WV_GLA)],
        out_shape=[f(W_FOX, n), f(W_FOX, n), f(W_FOX, n), f(W_FOX, n), f(H_FOX, n),
                   f(WK_GLA, n), f(WK_GLA, n), f(WK_GLA, n),
                   f(n, WV_GLA), f(n, WV_GLA)],
        compiler_params=_cparams(("arbitrary",)),
        name="inproj_sample",
    )(x, shift, scale, W["wr_s"], W["wt_s"], W["bf_col"], W["wgkT"], W["bgk_col"])


GLA_SAMPLE_BATCH = 8


def _gla_sample_kernel(n_stacked, qT_ref, kT_ref, laT_ref, v_ref, s_ref, *refs):
    snew_ref, o_ref = refs[n_stacked:]
    nb = qT_ref.shape[1]
    bs = s_ref.shape[0]
    qT = qT_ref[...]
    kT = kT_ref[...]
    aT = jnp.exp(laT_ref[...])
    lane = lax.broadcasted_iota(jnp.int32, (1, nb), 1)
    for u in range(bs):
        onehot = lane == pl.program_id(0) * bs + u
        pick = lambda a: jnp.sum(jnp.where(onehot, a, 0.0), axis=1, keepdims=True)
        q_col, k_col, a_col = pick(qT), pick(kT), pick(aT)
        v = v_ref[u]
        for hh in range(H_GLA):
            rows = slice(hh * DK_GLA, (hh + 1) * DK_GLA)
            cols = slice(hh * DV_GLA, (hh + 1) * DV_GLA)
            st = a_col[rows] * s_ref[u, hh] + k_col[rows] * v[:, cols]
            snew_ref[u, hh] = st
            o_ref[u, :, cols] = jnp.sum(q_col[rows] * st, axis=0, keepdims=True)


def _gla_sample(l, qT, kT, laT, v, state, stacked):
    nb = qT.shape[1]
    bs = GLA_SAMPLE_BATCH
    small = pl.BlockSpec((WK_GLA, nb), lambda g: (0, 0))
    lay_state = pl.BlockSpec((None, bs, H_GLA, DK_GLA, DV_GLA), lambda g: (l, g, 0, 0, 0))
    in_specs = [small, small, small, pl.BlockSpec((bs, 1, WV_GLA), lambda g: (g, 0, 0)), lay_state]
    args = [qT, kT, laT, v.reshape(nb, 1, WV_GLA), state]
    aliases = {}
    if stacked is not None:
        aliases = {len(args): 0}
        in_specs = in_specs + [pl.BlockSpec(memory_space=pl.ANY)]
        args = args + [stacked]
    return pl.pallas_call(
        functools.partial(_gla_sample_kernel, len(aliases)),
        grid=(nb // bs,),
        in_specs=in_specs,
        out_specs=[lay_state, pl.BlockSpec((bs, 1, WV_GLA), lambda g: (g, 0, 0))],
        out_shape=[jax.ShapeDtypeStruct((DEPTH, nb, H_GLA, DK_GLA, DV_GLA), F32),
                   jax.ShapeDtypeStruct((nb, 1, WV_GLA), F32)],
        input_output_aliases=aliases,
        compiler_params=_cparams(("arbitrary",)),
        name="gla_sample",
    )(*args)


def _prep_weights(w_in, b_f, w_gk, b_gk):
    tr = lambda a: jnp.transpose(a, (0, 2, 1))
    w_t = tr(w_in)
    sl = lambda off, n: w_t[:, off:off + n, :]
    qf = sl(_QF, W_FOX) * FOX_SCALE
    qg = sl(_QG, WK_GLA) * GLA_SCALE
    kf, vf, fl, gf = sl(_KF, W_FOX), sl(_VF, W_FOX), sl(_FL, H_FOX), sl(_GF, W_FOX)
    kg, vg, gl, gg = sl(_KG, WK_GLA), sl(_VG, WV_GLA), sl(_GL, GATE_RANK), sl(_GG, WV_GLA)
    zr = lambda n: jnp.zeros((DEPTH, n, D_MODEL), F32)
    fl_pad = jnp.concatenate([fl, zr(SUB_BF16 - H_FOX)], axis=1)
    glfl = jnp.concatenate([gl, fl, zr(LANES - GATE_RANK - H_FOX)], axis=1)
    wr_p = jnp.concatenate([kf, qg, kg, vg, gg, glfl], axis=1).astype(BF16)
    wt_p = jnp.concatenate([qf, kf, vf, gf, fl_pad], axis=1).astype(BF16)
    wr_s = tr(jnp.concatenate([vg, gg], axis=1)).astype(BF16)
    wt_s = jnp.concatenate([qf, kf, vf, gf, qg, kg, gl, fl_pad], axis=1).astype(BF16)
    zf = lambda n: jnp.zeros((DEPTH, n), F32)
    bf_col = jnp.concatenate([b_f, zf(SUB_BF16 - H_FOX)], axis=1)[:, :, None]
    bf_row = jnp.concatenate([zf(_FL_LANE), b_f, zf(LANES - _FL_LANE - H_FOX)], axis=1)[:, None, :]
    wgk_pad = jnp.concatenate([w_gk, jnp.zeros((DEPTH, LANES - GATE_RANK, WK_GLA), F32)], axis=1).astype(BF16)
    wgkT = tr(w_gk).astype(BF16)
    return dict(wr_p=wr_p, wt_p=wt_p, wr_s=wr_s, wt_s=wt_s, bf_col=bf_col, bf_row=bf_row,
                wgk_pad=wgk_pad, wgkT=wgkT, bgk_row=b_gk[:, None, :], bgk_col=b_gk[:, :, None])


def kernel(x_prompt, x_sample, cache_k, cache_v, cache_logf, state_gla, page_table, c_prompt, c_sample,
           w_ada, b_ada, w_in, b_f, w_gk, b_gk, gla_gain, w_out, ln_g, ln_b):
    B, S, _ = x_prompt.shape
    nb = x_sample.shape[0]
    W = _prep_weights(w_in, b_f, w_gk, b_gk)
    wo = w_out.astype(BF16)
    gain = gla_gain[:, None, :]
    lng = ln_g[:, None, :]
    lnb = ln_b[:, None, :]

    mod = _ada_all(jnp.concatenate([c_prompt, c_sample], axis=0), w_ada, b_ada)
    ck = jnp.transpose(cache_k, (0, 1, 3, 4, 2))
    cv = jnp.transpose(cache_v, (0, 1, 3, 4, 2))
    cf = jnp.transpose(cache_logf, (0, 1, 3, 2))

    xp = x_prompt
    xs = x_sample.reshape(1, nb, D_MODEL)
    sp_l, ks_l, vs_l, fs_l = ([] for _ in range(4))
    stacked_p = None
    stacked_s = None
    for l in range(DEPTH):
        shift_p = mod[l, :B, None, 0:D_MODEL]
        scale_p = mod[l, :B, None, D_MODEL:2 * D_MODEL]
        gate_p = mod[l, :B, None, 2 * D_MODEL:]
        shift_s = mod[l, B:, 0:D_MODEL]
        scale_s = mod[l, B:, D_MODEL:2 * D_MODEL]
        gate_s = mod[l, B:, 2 * D_MODEL:][None]

        (qT, ka, vTb, kT_all, vT_all, lfT_all, gfT, qg, kg, vg, la, gg) = _inproj_prompt(
            l, xp, shift_p, scale_p, W, stacked_p)
        stacked_p = (kT_all, vT_all, lfT_all)
        (qfT, kfT, vfT, gfTs, lfT, qgT, kgT, laT, vgs, ggs) = _inproj_sample(l, xs[0], shift_s, scale_s, W)
        o_fT = _fox_prompt(qT, ka, vTb)
        o_g, s_p, o_fTs = _gla_decode(l, qg, kg, vg, la, page_table, qfT, kfT, vfT, lfT, ck, cv, cf)
        xp = _merge(l, xp, o_fT, gfT, o_g, gg, gate_p, gain, wo, lng, lnb, tm=min(512, S), per_row_gate=False)
        sp_l.append(s_p)

        stacked_s, o_gs = _gla_sample(l, qgT, kgT, laT, vgs, state_gla, stacked_s)
        xs = _merge(l, xs, o_fTs[None], gfTs[None], o_gs.reshape(1, nb, WV_GLA), ggs[None], gate_s,
                    gain, wo, lng, lnb, tm=nb, per_row_gate=True)
        ks_l.append(kfT); vs_l.append(vfT); fs_l.append(lfT)

    kp = jnp.transpose(stacked_p[0].reshape(DEPTH, B, H_FOX, HD_FOX, S), (0, 1, 4, 2, 3))
    vp = jnp.transpose(stacked_p[1].reshape(DEPTH, B, H_FOX, HD_FOX, S), (0, 1, 4, 2, 3))
    fp = jnp.transpose(stacked_p[2], (0, 1, 3, 2))
    ks = jnp.transpose(jnp.stack(ks_l).reshape(DEPTH, H_FOX, HD_FOX, nb), (0, 3, 1, 2))[:, :, None]
    vs = jnp.transpose(jnp.stack(vs_l).reshape(DEPTH, H_FOX, HD_FOX, nb), (0, 3, 1, 2))[:, :, None]
    fs = jnp.transpose(jnp.stack(fs_l), (0, 2, 1))[:, :, None]
    return (xp, xs.reshape(nb, 1, D_MODEL), kp, vp, fp, jnp.stack(sp_l), ks, vs, fs, stacked_s)
```
